```python
import math
import jax, jax.numpy as jnp
from jax import lax
import numpy as np

D_MODEL = 2048
BATCH = 1
SEQ = 8192
DEPTH = 4

EPS = 1e-6
N_BRANCH = 3
BRANCH_WIDTH = 1024

MLA_HEADS = 8
QK_NOPE = 128
QK_ROPE = 64
V_HEAD = 128
Q_LORA = 512
KV_LORA = 512
ROPE_THETA = 10000.0
Q_BLOCK = 128

LRU_WIDTH = 1024
LRU_BLOCKS = 8
LRU_BLOCK_W = LRU_WIDTH // LRU_BLOCKS
LRU_C = 8.0
CONV_W = 4

GDN_HEADS = 8
GDN_DK = 128
GDN_DV = 128
GDN_CHUNK = 64

FFN_HIDDEN = -(-(8 * D_MODEL) // (3 * 256)) * 256

IN_SIZES = [Q_LORA, KV_LORA, QK_ROPE,
            LRU_WIDTH, LRU_WIDTH,
            GDN_HEADS * GDN_DK, GDN_HEADS * GDN_DK, GDN_HEADS * GDN_DV, GDN_HEADS * GDN_DV, GDN_HEADS, GDN_HEADS,
            N_BRANCH * D_MODEL]
IN_COLS = sum(IN_SIZES)
IN_SPLITS = [sum(IN_SIZES[:i]) for i in range(1, len(IN_SIZES))]

kernel_name = 'hybrid_mla_rglru_gdn_gated_merge'


def rmsnorm(x, w):
    xf = x.astype(jnp.float32)
    y = xf * lax.rsqrt(jnp.mean(xf * xf, axis=-1, keepdims=True) + EPS)
    return (y * w.astype(jnp.float32)).astype(x.dtype)


def causal_depthwise_conv(x, w):
    c = x.shape[-1]
    return lax.conv_general_dilated(
        x, w.astype(x.dtype)[:, None, :], window_strides=(1,), padding=((CONV_W - 1, 0),),
        dimension_numbers=('NWC', 'WIO', 'NWC'), feature_group_count=c)


def apply_rope(x, positions):
    half = QK_ROPE // 2
    inv_freq = ROPE_THETA ** (-jnp.arange(half, dtype=jnp.float32) / half)
    ang = positions.astype(jnp.float32)[:, :, None, None] * inv_freq
    cos, sin = jnp.cos(ang), jnp.sin(ang)
    xf = x.astype(jnp.float32)
    x1, x2 = xf[..., :half], xf[..., half:]
    return jnp.concatenate([x1 * cos - x2 * sin, x2 * cos + x1 * sin], axis=-1).astype(x.dtype)


def block_causal_attention(q, k, v):
    b, s, h, dqk = q.shape
    dv = v.shape[-1]
    nblk = s // Q_BLOCK
    scale = dqk ** -0.5
    qb = jnp.moveaxis(q.reshape(b, nblk, Q_BLOCK, h, dqk), 1, 0)
    starts = jnp.arange(nblk, dtype=jnp.int32) * Q_BLOCK
    kpos = jnp.arange(s, dtype=jnp.int32)

    def one_block(args):
        q_blk, start = args
        sc = jnp.einsum('bqhd,bkhd->bhqk', q_blk, k, preferred_element_type=jnp.float32) * scale
        qpos = start + jnp.arange(Q_BLOCK, dtype=jnp.int32)
        sc = jnp.where(kpos[None, :] <= qpos[:, None], sc, -jnp.inf)
        p = jax.nn.softmax(sc, axis=-1).astype(v.dtype)
        return jnp.einsum('bhqk,bkhd->bqhd', p, v)

    out = lax.map(one_block, (qb, starts))
    return jnp.moveaxis(out, 0, 1).reshape(b, s, h, dv)


def mla_branch(c_q, c_kv, k_rope, positions, q_norm, w_uq, kv_norm, w_ukv):
    b, s, _ = c_q.shape
    q = (rmsnorm(c_q, q_norm) @ w_uq).reshape(b, s, MLA_HEADS, QK_NOPE + QK_ROPE)
    q = jnp.concatenate([q[..., :QK_NOPE], apply_rope(q[..., QK_NOPE:], positions)], axis=-1)
    kv = (rmsnorm(c_kv, kv_norm) @ w_ukv).reshape(b, s, MLA_HEADS, QK_NOPE + V_HEAD)
    k_nope, v = kv[..., :QK_NOPE], kv[..., QK_NOPE:]
    k_pe = apply_rope(k_rope[:, :, None, :], positions)
    k = jnp.concatenate([k_nope, jnp.broadcast_to(k_pe, (b, s, MLA_HEADS, QK_ROPE))], axis=-1)
    o = block_causal_attention(q, k, v)
    return o.reshape(b, s, MLA_HEADS * V_HEAD)


def _lru_combine(e1, e2):
    a1, b1 = e1
    a2, b2 = e2
    return a1 * a2, a2 * b1 + b2


def rglru_branch(x_in, y_in, conv_w, conv_b, w_a, b_a, w_x, b_x, lam):
    b, s, _ = x_in.shape
    xc = causal_depthwise_conv(x_in, conv_w) + conv_b
    xb = xc.reshape(b, s, LRU_BLOCKS, LRU_BLOCK_W)
    r = jax.nn.sigmoid(jnp.einsum('bsgi,gij->bsgj', xb, w_a).reshape(b, s, LRU_WIDTH) + b_a)
    gi = jax.nn.sigmoid(jnp.einsum('bsgi,gij->bsgj', xb, w_x).reshape(b, s, LRU_WIDTH) + b_x)
    log_a = -LRU_C * r.astype(jnp.float32) * jax.nn.softplus(-lam.astype(jnp.float32))
    a = jnp.exp(log_a)
    mult = jnp.sqrt(-jnp.expm1(2.0 * log_a))
    bx = mult * (gi * xc).astype(jnp.float32)
    _, h = lax.associative_scan(_lru_combine, (a, bx), axis=1)
    return h.astype(x_in.dtype) * jax.nn.gelu(y_in)


def _l2norm(t):
    return t * lax.rsqrt(jnp.sum(t * t, axis=-1, keepdims=True) + EPS)


def chunk_gated_delta_rule(q, k, v, g, beta):
    b, s, h, dk = q.shape
    dv = v.shape[-1]
    n = s // GDN_CHUNK

    def chunks(t):
        t = t.reshape((b, n, GDN_CHUNK, h) + t.shape[3:])
        return jnp.moveaxis(t, 3, 1)

    q = chunks(q) * dk ** -0.5
    k, v, g, beta = chunks(k), chunks(v), chunks(g), chunks(beta)
    gc = jnp.cumsum(g, axis=-1)
    idx = jnp.arange(GDN_CHUNK)
    incl = idx[:, None] >= idx[None, :]
    strict = idx[:, None] > idx[None, :]
    decay = jnp.exp(jnp.where(incl, gc[..., :, None] - gc[..., None, :], -jnp.inf))
    k_beta = k * beta[..., None]
    v_beta = v * beta[..., None]
    kkt = jnp.einsum('bhnid,bhnjd->bhnij', k_beta, k) * decay
    a_mat = jnp.where(strict, kkt, 0.0) + jnp.eye(GDN_CHUNK, dtype=q.dtype)
    u = lax.linalg.triangular_solve(a_mat, v_beta, left_side=True, lower=True, unit_diagonal=True)
    w = lax.linalg.triangular_solve(a_mat, k_beta * jnp.exp(gc)[..., None], left_side=True, lower=True, unit_diagonal=True)
    qk = jnp.where(incl, jnp.einsum('bhnid,bhnjd->bhnij', q, k) * decay, 0.0)
    q_dec = q * jnp.exp(gc)[..., None]
    k_tail = k * jnp.exp(gc[..., -1:] - gc)[..., None]
    g_tail = jnp.exp(gc[..., -1])
    xs = (jnp.moveaxis(u, 2, 0), jnp.moveaxis(w, 2, 0), jnp.moveaxis(qk, 2, 0),
          jnp.moveaxis(q_dec, 2, 0), jnp.moveaxis(k_tail, 2, 0), jnp.moveaxis(g_tail, 2, 0))

    def step(state, inp):
        u_n, w_n, qk_n, qd_n, kt_n, gt_n = inp
        v_new = u_n - jnp.einsum('bhid,bhde->bhie', w_n, state)
        o_n = jnp.einsum('bhid,bhde->bhie', qd_n, state) + jnp.einsum('bhij,bhje->bhie', qk_n, v_new)
        state = state * gt_n[..., None, None] + jnp.einsum('bhid,bhie->bhde', kt_n, v_new)
        return state, o_n

    state0 = jnp.zeros((b, h, dk, dv), q.dtype)
    _, o = lax.scan(step, state0, xs)
    o = jnp.moveaxis(o, 0, 2)
    return jnp.moveaxis(o, 1, 3).reshape(b, s, h, dv)


def gdn_branch(q, k, v, z, b_logit, a_logit, conv_w, a_log, dt_bias, norm_w):
    b, s, _ = q.shape
    qkv = jax.nn.silu(causal_depthwise_conv(jnp.concatenate([q, k, v], axis=-1), conv_w))
    qc, kc, vc = jnp.split(qkv, [GDN_HEADS * GDN_DK, 2 * GDN_HEADS * GDN_DK], axis=-1)
    qc = _l2norm(qc.reshape(b, s, GDN_HEADS, GDN_DK).astype(jnp.float32))
    kc = _l2norm(kc.reshape(b, s, GDN_HEADS, GDN_DK).astype(jnp.float32))
    vc = vc.reshape(b, s, GDN_HEADS, GDN_DV).astype(jnp.float32)
    beta = jax.nn.sigmoid(b_logit.astype(jnp.float32))
    g = -jnp.exp(a_log.astype(jnp.float32)) * jax.nn.softplus(a_logit.astype(jnp.float32) + dt_bias.astype(jnp.float32))
    o = chunk_gated_delta_rule(qc, kc, vc, g, beta)
    o = rmsnorm(o, norm_w) * jax.nn.silu(z.reshape(b, s, GDN_HEADS, GDN_DV).astype(jnp.float32))
    return o.reshape(b, s, GDN_HEADS * GDN_DV).astype(q.dtype)


def setup_inputs(seed: int = 0) -> dict:
    key = jax.random.key(seed)
    ks = jax.random.split(key, 32)
    f32 = jnp.float32

    def nrm(k, shape, fan_in):
        return jax.random.normal(k, shape, f32) * fan_in ** -0.5

    def gain(k, shape):
        return 1.0 + 0.02 * jax.random.normal(k, shape, f32)

    def small(k, shape):
        return 0.01 * jax.random.normal(k, shape, f32)

    x = jax.random.normal(ks[0], (BATCH, SEQ, D_MODEL), f32)
    positions = jnp.broadcast_to(jnp.arange(SEQ, dtype=jnp.int32), (BATCH, SEQ))
    u = jax.random.uniform(ks[13], (DEPTH, LRU_WIDTH), f32, 0.9, 0.999)
    a0 = u ** (1.0 / LRU_C)
    lru_lambda = jnp.log(a0) - jnp.log1p(-a0)
    gdn_a_log = jnp.log(jax.random.uniform(ks[15], (DEPTH, GDN_HEADS), f32, 1.0, 16.0))
    dt = jnp.exp(jax.random.uniform(ks[16], (DEPTH, GDN_HEADS), f32, math.log(1e-3), math.log(1e-1)))
    gdn_dt_bias = dt + jnp.log(-jnp.expm1(-dt))
    return {
        'x': x,
        'positions': positions,
        'norm_mix': gain(ks[1], (DEPTH, D_MODEL)),
        'w_in': nrm(ks[2], (DEPTH, D_MODEL, IN_COLS), D_MODEL),
        'mla_q_norm': gain(ks[3], (DEPTH, Q_LORA)),
        'mla_w_uq': nrm(ks[4], (DEPTH, Q_LORA, MLA_HEADS * (QK_NOPE + QK_ROPE)), Q_LORA),
        'mla_kv_norm': gain(ks[5], (DEPTH, KV_LORA)),
        'mla_w_ukv': nrm(ks[6], (DEPTH, KV_LORA, MLA_HEADS * (QK_NOPE + V_HEAD)), KV_LORA),
        'lru_conv_w': nrm(ks[7], (DEPTH, CONV_W, LRU_WIDTH), CONV_W),
        'lru_conv_b': small(ks[8], (DEPTH, LRU_WIDTH)),
        'lru_w_a': nrm(ks[9], (DEPTH, LRU_BLOCKS, LRU_BLOCK_W, LRU_BLOCK_W), LRU_BLOCK_W),
        'lru_b_a': small(ks[10], (DEPTH, LRU_WIDTH)),
        'lru_w_x': nrm(ks[11], (DEPTH, LRU_BLOCKS, LRU_BLOCK_W, LRU_BLOCK_W), LRU_BLOCK_W),
        'lru_b_x': small(ks[12], (DEPTH, LRU_WIDTH)),
        'lru_lambda': lru_lambda,
        'gdn_conv_w': nrm(ks[14], (DEPTH, CONV_W, 2 * GDN_HEADS * GDN_DK + GDN_HEADS * GDN_DV), CONV_W),
        'gdn_a_log': gdn_a_log,
        'gdn_dt_bias': gdn_dt_bias,
        'gdn_norm': gain(ks[17], (DEPTH, GDN_DV)),
        'w_branch': nrm(ks[18], (DEPTH, N_BRANCH, BRANCH_WIDTH, D_MODEL), BRANCH_WIDTH),
        'b_gate': small(ks[19], (DEPTH, N_BRANCH, D_MODEL)),
        'w_out': nrm(ks[20], (DEPTH, D_MODEL, D_MODEL), D_MODEL),
        'norm_ffn': gain(ks[21], (DEPTH, D_MODEL)),
        'ffn_w_gate': nrm(ks[22], (DEPTH, D_MODEL, FFN_HIDDEN), D_MODEL),
        'ffn_w_up': nrm(ks[23], (DEPTH, D_MODEL, FFN_HIDDEN), D_MODEL),
        'ffn_w_down': nrm(ks[24], (DEPTH, FFN_HIDDEN, D_MODEL), FFN_HIDDEN),
        'norm_final': gain(ks[25], (D_MODEL,)),
    }


def reference(x, positions, norm_mix, w_in, mla_q_norm, mla_w_uq, mla_kv_norm, mla_w_ukv,
              lru_conv_w, lru_conv_b, lru_w_a, lru_b_a, lru_w_x, lru_b_x, lru_lambda,
              gdn_conv_w, gdn_a_log, gdn_dt_bias, gdn_norm, w_branch, b_gate, w_out,
              norm_ffn, ffn_w_gate, ffn_w_up, ffn_w_down, norm_final):
    b, s, _ = x.shape
    for l in range(DEPTH):
        h = rmsnorm(x, norm_mix[l])
        proj = h @ w_in[l]
        (c_q, c_kv, k_rope, lru_x, lru_y, g_q, g_k, g_v, g_z, g_b, g_a, gate_logits) = jnp.split(proj, IN_SPLITS, axis=-1)
        y_mla = mla_branch(c_q, c_kv, k_rope, positions, mla_q_norm[l], mla_w_uq[l], mla_kv_norm[l], mla_w_ukv[l])
        y_lru = rglru_branch(lru_x, lru_y, lru_conv_w[l], lru_conv_b[l], lru_w_a[l], lru_b_a[l],
                             lru_w_x[l], lru_b_x[l], lru_lambda[l])
        y_gdn = gdn_branch(g_q, g_k, g_v, g_z, g_b, g_a, gdn_conv_w[l], gdn_a_log[l], gdn_dt_bias[l], gdn_norm[l])
        branches = jnp.stack([y_mla, y_lru, y_gdn], axis=2)
        up = jnp.einsum('bsnw,nwd->bsnd', branches, w_branch[l])
        gates = jax.nn.sigmoid(gate_logits.reshape(b, s, N_BRANCH, D_MODEL) + b_gate[l])
        mixed = jnp.sum(gates * up, axis=2)
        x = x + mixed @ w_out[l]
        h = rmsnorm(x, norm_ffn[l])
        x = x + (jax.nn.silu(h @ ffn_w_gate[l]) * (h @ ffn_w_up[l])) @ ffn_w_down[l]
    return rmsnorm(x, norm_final)
```

```python
import functools
import math

import jax
import jax.numpy as jnp
from jax import lax
from jax.experimental import pallas as pl
from jax.experimental.pallas import tpu as pltpu

F32 = jnp.float32
BF16 = jnp.bfloat16
HIGHEST = lax.Precision.HIGHEST

EPS = 1e-6
N_BRANCH = 3
MLA_HEADS = 8
QK_NOPE = 128
QK_ROPE = 64
ROPE_HALF = QK_ROPE // 2
QK_DIM = QK_NOPE + QK_ROPE
V_HEAD = 128
Q_LORA = 512
KV_LORA = 512
ROPE_THETA = 10000.0
LRU_WIDTH = 1024
LRU_BLOCKS = 8
LRU_BLOCK_W = LRU_WIDTH // LRU_BLOCKS
LRU_C = 8.0
CONV_W = 4
GDN_HEADS = 8
GDN_DK = 128
GDN_DV = 128
GDN_CHUNK = 64
GDN_INV_BLOCK = 16

V7X_VMEM_BYTES = 64 * 1024 * 1024
VMEM_LIMIT_BYTES = V7X_VMEM_BYTES - 8 * 1024 * 1024
SUBLANES = 8
LANES = 128

GATE_COLS = N_BRANCH * 2048
GDN_QKV_COLS = 3 * GDN_HEADS * GDN_DK
SMALL_COLS = 1152
SMALL_ROPE_OFF = Q_LORA + KV_LORA
SMALL_B_OFF = SMALL_ROPE_OFF + QK_ROPE
SMALL_A_OFF = SMALL_B_OFF + GDN_HEADS


def _params(*semantics):
    return pltpu.CompilerParams(dimension_semantics=semantics, vmem_limit_bytes=VMEM_LIMIT_BYTES)


def _dot(a, b, precision=None):
    return jnp.dot(a, b, preferred_element_type=F32, precision=precision)


def _rms(x, gain):
    return x * lax.rsqrt(jnp.mean(x * x, axis=-1, keepdims=True) + EPS) * gain


def _softplus(x):
    return jnp.maximum(x, 0.0) + jnp.log1p(jnp.exp(-jnp.abs(x)))


def _silu(x):
    return x * jax.nn.sigmoid(x)


def _gelu_tanh(x):
    return 0.5 * x * (1.0 + jnp.tanh(math.sqrt(2.0 / math.pi) * (x + 0.044715 * (x * x * x))))


def _norm_matmul_kernel(x_ref, g_ref, w_ref, o_ref, h_ref):
    @pl.when(pl.program_id(1) == 0)
    def _():
        h_ref[...] = _rms(x_ref[...], g_ref[...]).astype(BF16)

    o_ref[...] = _dot(h_ref[...], w_ref[...]).astype(o_ref.dtype)


def norm_matmul(x, gain, w, *, tm, tn):
    s, k = x.shape
    n = w.shape[1]
    return pl.pallas_call(
        _norm_matmul_kernel,
        grid=(s // tm, n // tn),
        in_specs=[
            pl.BlockSpec((tm, k), lambda i, j: (i, 0)),
            pl.BlockSpec((1, k), lambda i, j: (0, 0)),
            pl.BlockSpec((k, tn), lambda i, j: (0, j)),
        ],
        out_specs=pl.BlockSpec((tm, tn), lambda i, j: (i, j)),
        out_shape=jax.ShapeDtypeStruct((s, n), F32),
        scratch_shapes=[pltpu.VMEM((tm, k), BF16)],
        compiler_params=_params("arbitrary", "arbitrary"),
        name="norm_matmul",
    )(x, gain, w)


def _rope_table_kernel(pos_ref, inv_ref, cos_ref, sin_ref):
    ang = pos_ref[...] * inv_ref[...]
    cos_ref[...] = jnp.cos(ang)
    sin_ref[...] = jnp.sin(ang)


def rope_tables(pos_b, inv_b, *, tm):
    s, n = pos_b.shape
    blk = pl.BlockSpec((tm, n), lambda i: (i, 0))
    return pl.pallas_call(
        _rope_table_kernel,
        grid=(s // tm,),
        in_specs=[blk, pl.BlockSpec((1, n), lambda i: (0, 0))],
        out_specs=[blk, blk],
        out_shape=[jax.ShapeDtypeStruct((s, n), F32)] * 2,
        compiler_params=_params("arbitrary"),
        name="rope_tables",
    )(pos_b, inv_b)


def _mla_prep_kernel(p_ref, qg_ref, kvg_ref, wqn_ref, wq1_ref, wq2_ref, wk_ref, wv_ref,
                     cos_ref, sin_ref, q_out, k_out, v_out):
    p = p_ref[...]
    hq = _rms(p[:, :Q_LORA], qg_ref[...]).astype(BF16)
    hkv = _rms(p[:, Q_LORA:SMALL_ROPE_OFF], kvg_ref[...]).astype(BF16)
    kr = p[:, SMALL_ROPE_OFF:SMALL_ROPE_OFF + QK_ROPE]
    cos = cos_ref[...]
    sin = sin_ref[...]
    qn = _dot(hq, wqn_ref[...])
    q1 = _dot(hq, wq1_ref[...])
    q2 = _dot(hq, wq2_ref[...])
    qr1 = q1 * cos - q2 * sin
    qr2 = q2 * cos + q1 * sin
    kn = _dot(hkv, wk_ref[...])
    v = _dot(hkv, wv_ref[...])
    c32 = cos[:, :ROPE_HALF]
    s32 = sin[:, :ROPE_HALF]
    k1 = kr[:, :ROPE_HALF]
    k2 = kr[:, ROPE_HALF:]
    kr1 = k1 * c32 - k2 * s32
    kr2 = k2 * c32 + k1 * s32
    for h in range(MLA_HEADS):
        nope = slice(h * QK_NOPE, (h + 1) * QK_NOPE)
        rope = slice(h * ROPE_HALF, (h + 1) * ROPE_HALF)
        q_out[h] = jnp.concatenate([qn[:, nope], qr1[:, rope], qr2[:, rope]], axis=-1).astype(BF16)
        k_out[h] = jnp.concatenate([kn[:, nope], kr1, kr2], axis=-1).astype(BF16)
        v_out[h] = v[:, h * V_HEAD:(h + 1) * V_HEAD].astype(BF16)


def mla_prep(proj_small, q_gain, kv_gain, wqn, wq1, wq2, wk, wv, cos, sin, *, tm):
    s = proj_small.shape[0]
    full = lambda a: pl.BlockSpec(a.shape, lambda i: (0,) * a.ndim)
    rows = lambda n: pl.BlockSpec((tm, n), lambda i: (i, 0))
    heads = lambda d: pl.BlockSpec((MLA_HEADS, tm, d), lambda i: (0, i, 0))
    return pl.pallas_call(
        _mla_prep_kernel,
        grid=(s // tm,),
        in_specs=[rows(SMALL_COLS), full(q_gain), full(kv_gain), full(wqn), full(wq1), full(wq2),
                  full(wk), full(wv), rows(cos.shape[1]), rows(sin.shape[1])],
        out_specs=[heads(QK_DIM), heads(QK_DIM), heads(V_HEAD)],
        out_shape=[jax.ShapeDtypeStruct((MLA_HEADS, s, QK_DIM), BF16),
                   jax.ShapeDtypeStruct((MLA_HEADS, s, QK_DIM), BF16),
                   jax.ShapeDtypeStruct((MLA_HEADS, s, V_HEAD), BF16)],
        compiler_params=_params("arbitrary"),
        name="mla_prep",
    )(proj_small, q_gain, kv_gain, wqn, wq1, wq2, wk, wv, cos, sin)


def _attention_kernel(q_ref, k_ref, v_ref, o_ref, *, tq, exp2_scale):
    i = pl.program_id(1)
    q = q_ref[0]

    def step(j, carry, diagonal):
        m, l, acc = carry
        start = pl.multiple_of(j * tq, tq)
        kj = k_ref[0, pl.ds(start, tq), :]
        vj = v_ref[0, pl.ds(start, tq), :]
        s = lax.dot_general(q, kj, (((1,), (1,)), ((), ())), preferred_element_type=F32)
        if diagonal:
            row = lax.broadcasted_iota(jnp.int32, (tq, tq), 0)
            col = lax.broadcasted_iota(jnp.int32, (tq, tq), 1)
            s = jnp.where(col <= row, s, -jnp.inf)
        m_new = jnp.maximum(m, jnp.max(s, axis=-1, keepdims=True))
        p = jnp.exp2((s - m_new) * exp2_scale)
        alpha = jnp.exp2((m - m_new) * exp2_scale)
        l = alpha * l + jnp.sum(p, axis=-1, keepdims=True)
        acc = alpha * acc + _dot(p.astype(BF16), vj)
        return m_new, l, acc

    init = (jnp.full((tq, 1), -jnp.inf, F32), jnp.zeros((tq, 1), F32), jnp.zeros((tq, V_HEAD), F32))
    carry = lax.fori_loop(0, i, lambda j, c: step(j, c, False), init)
    _, l, acc = step(i, carry, True)
    o_ref[...] = (acc / l).astype(o_ref.dtype)


def attention(q, k, v, *, tq):
    h, s, _ = q.shape
    exp2_scale = QK_DIM ** -0.5 * math.log2(math.e)
    return pl.pallas_call(
        functools.partial(_attention_kernel, tq=tq, exp2_scale=exp2_scale),
        grid=(h, s // tq),
        in_specs=[
            pl.BlockSpec((1, tq, QK_DIM), lambda hh, i: (hh, i, 0)),
            pl.BlockSpec((1, s, QK_DIM), lambda hh, i: (hh, 0, 0)),
            pl.BlockSpec((1, s, V_HEAD), lambda hh, i: (hh, 0, 0)),
        ],
        out_specs=pl.BlockSpec((tq, V_HEAD), lambda hh, i: (i, hh)),
        out_shape=jax.ShapeDtypeStruct((s, h * V_HEAD), BF16),
        compiler_params=_params("arbitrary", "arbitrary"),
        name="attention",
    )(q, k, v)


def _causal_conv_tile(x_ref, cw_ref, xbuf, tm):
    @pl.when(pl.program_id(0) == 0)
    def _():
        xbuf[0:SUBLANES, :] = jnp.zeros((SUBLANES, xbuf.shape[1]), F32)

    xbuf[SUBLANES:SUBLANES + tm, :] = x_ref[...]
    cw = cw_ref[...]
    acc = cw[0:1, :] * xbuf[pl.ds(SUBLANES - CONV_W + 1, tm), :]
    for k in range(1, CONV_W):
        acc = acc + cw[k:k + 1, :] * xbuf[pl.ds(SUBLANES - CONV_W + 1 + k, tm), :]
    xbuf[0:SUBLANES, :] = xbuf[tm:tm + SUBLANES, :]
    return acc


def _rglru_kernel(x_ref, y_ref, cw_ref, cb_ref, wg_ref, ba_ref, bx_ref, lam_ref, o_ref,
                  xbuf, a_s, b_s, h_carry, *, tm):
    @pl.when(pl.program_id(0) == 0)
    def _():
        h_carry[...] = jnp.zeros(h_carry.shape, F32)

    xc = _causal_conv_tile(x_ref, cw_ref, xbuf, tm) + cb_ref[...]
    xcb = xc.astype(BF16)
    pair_w = 2 * LRU_BLOCK_W
    r_parts, i_parts = [], []
    for p in range(LRU_BLOCKS // 2):
        g = _dot(xcb[:, p * pair_w:(p + 1) * pair_w], wg_ref[p])
        r_parts.append(g[:, :pair_w])
        i_parts.append(g[:, pair_w:])
    r = jax.nn.sigmoid(jnp.concatenate(r_parts, axis=-1) + ba_ref[...])
    gi = jax.nn.sigmoid(jnp.concatenate(i_parts, axis=-1) + bx_ref[...])
    log_a = -LRU_C * r * _softplus(-lam_ref[...])
    t = jnp.tanh(log_a)
    mult = jnp.sqrt(-2.0 * t / (1.0 - t))
    a_s[...] = jnp.exp(log_a)
    b_s[...] = mult * (gi * xc)

    row = lax.broadcasted_iota(jnp.int32, (SUBLANES, LRU_WIDTH), 0)

    def group(gidx, h_prev):
        r0 = pl.multiple_of(gidx * SUBLANES, SUBLANES)
        a = a_s[pl.ds(r0, SUBLANES), :]
        b = b_s[pl.ds(r0, SUBLANES), :]
        shift = 1
        while shift < SUBLANES:
            keep = row >= shift
            a_sh = jnp.where(keep, pltpu.roll(a, shift, 0), 1.0)
            b_sh = jnp.where(keep, pltpu.roll(b, shift, 0), 0.0)
            b = a * b_sh + b
            a = a * a_sh
            shift *= 2
        hs = a * h_prev + b
        b_s[pl.ds(r0, SUBLANES), :] = hs
        return hs[SUBLANES - 1:SUBLANES, :]

    h_carry[...] = lax.fori_loop(0, tm // SUBLANES, group, h_carry[...])
    o_ref[...] = (b_s[...] * _gelu_tanh(y_ref[...])).astype(o_ref.dtype)


def rglru(proj, x_col, y_col, conv_w, conv_b, w_gates, b_a, b_x, lam, *, tm):
    s = proj.shape[0]
    full = lambda a: pl.BlockSpec(a.shape, lambda i: (0,) * a.ndim)
    return pl.pallas_call(
        functools.partial(_rglru_kernel, tm=tm),
        grid=(s // tm,),
        in_specs=[
            pl.BlockSpec((tm, LRU_WIDTH), lambda i: (i, x_col)),
            pl.BlockSpec((tm, LRU_WIDTH), lambda i: (i, y_col)),
            full(conv_w), full(conv_b), full(w_gates), full(b_a), full(b_x), full(lam),
        ],
        out_specs=pl.BlockSpec((tm, LRU_WIDTH), lambda i: (i, 0)),
        out_shape=jax.ShapeDtypeStruct((s, LRU_WIDTH), BF16),
        scratch_shapes=[
            pltpu.VMEM((tm + SUBLANES, LRU_WIDTH), F32),
            pltpu.VMEM((tm, LRU_WIDTH), F32),
            pltpu.VMEM((tm, LRU_WIDTH), F32),
            pltpu.VMEM((1, LRU_WIDTH), F32),
        ],
        compiler_params=_params("arbitrary"),
        name="rglru",
    )(proj, proj, conv_w, conv_b, w_gates, b_a, b_x, lam)


def _unit_lower_inverse(l_mat, block_diag, eye, precision):
    mm = lambda a, b: _dot(a, b, precision)
    d = jnp.where(block_diag, l_mat, 0.0)
    n = l_mat - d
    d2 = mm(d, d)
    d4 = mm(d2, d2)
    d8 = mm(d4, d4)
    x = -d
    for dp in (d2, d4, d8):
        x = x + dp + mm(x, dp)
    m = n + mm(x, n)
    m2 = mm(m, m)
    y = m2 - m - mm(m, m2)
    return eye + x + y + mm(y, x)


def _gdn_kernel(qkv_ref, z_ref, ba_ref, cw_ref, alog_ref, dtb_ref, nw_ref, o_ref,
                xbuf, q_s, k_s, v_s, g_s, beta_s, state, *, tm, inv_precision):
    @pl.when(pl.program_id(0) == 0)
    def _():
        state[...] = jnp.zeros(state.shape, F32)

    act = _silu(_causal_conv_tile(qkv_ref, cw_ref, xbuf, tm))
    kw = GDN_HEADS * GDN_DK
    for h in range(GDN_HEADS):
        sl = slice(h * GDN_DK, (h + 1) * GDN_DK)
        qh = act[:, h * GDN_DK:(h + 1) * GDN_DK]
        kh = act[:, kw + h * GDN_DK:kw + (h + 1) * GDN_DK]
        q_s[:, sl] = qh * lax.rsqrt(jnp.sum(qh * qh, axis=-1, keepdims=True) + EPS) * (GDN_DK ** -0.5)
        k_s[:, sl] = kh * lax.rsqrt(jnp.sum(kh * kh, axis=-1, keepdims=True) + EPS)
    v_s[...] = act[:, 2 * kw:]
    ba = ba_ref[...]
    beta_s[...] = jax.nn.sigmoid(ba[:, SMALL_B_OFF % LANES:SMALL_B_OFF % LANES + GDN_HEADS])
    a_logit = ba[:, SMALL_A_OFF % LANES:SMALL_A_OFF % LANES + GDN_HEADS]
    g_s[...] = -jnp.exp(alog_ref[...]) * _softplus(a_logit + dtb_ref[...])

    c = GDN_CHUNK
    row = lax.broadcasted_iota(jnp.int32, (c, c), 0)
    col = lax.broadcasted_iota(jnp.int32, (c, c), 1)
    incl = row >= col
    strict = row > col
    eye = (row == col).astype(F32)
    tril = incl.astype(F32)
    stril = strict.astype(F32)
    block_diag = jnp.logical_and(strict, (row // GDN_INV_BLOCK) == (col // GDN_INV_BLOCK))
    norm_w = nw_ref[...]
    nt_dims = (((1,), (1,)), ((), ()))
    tn_dims = (((0,), (0,)), ((), ()))

    def chunk(ci, carry):
        r0 = pl.multiple_of(ci * c, c)
        rows = pl.ds(r0, c)
        g_c = g_s[rows, :]
        beta_c = beta_s[rows, :]
        gc = _dot(tril, g_c, HIGHEST)
        gb = jnp.concatenate([jnp.broadcast_to(g_c[:, h:h + 1], (c, c)) * stril
                              for h in range(GDN_HEADS)], axis=1)
        diff = _dot(tril, gb, HIGHEST)
        e_gc = jnp.exp(gc)
        g_last = gc[c - 1:c, :]
        e_tail = jnp.exp(g_last - gc)
        e_last = jnp.exp(g_last)
        for h in range(GDN_HEADS):
            sl = slice(h * GDN_DK, (h + 1) * GDN_DK)
            qh = q_s[rows, sl]
            kh = k_s[rows, sl]
            vh = v_s[rows, sl]
            bh = jnp.broadcast_to(beta_c[:, h:h + 1], (c, GDN_DK))
            eg = jnp.broadcast_to(e_gc[:, h:h + 1], (c, GDN_DK))
            et = jnp.broadcast_to(e_tail[:, h:h + 1], (c, GDN_DK))
            kb = kh * bh
            vb = vh * bh
            decay = jnp.where(incl, jnp.exp(diff[:, h * c:(h + 1) * c]), 0.0)
            khb = kh.astype(BF16)
            kq = lax.dot_general(jnp.concatenate([kb, qh], axis=0).astype(BF16), khb, nt_dims,
                                 preferred_element_type=F32)
            l_mat = jnp.where(strict, kq[:c] * decay, 0.0)
            qk = kq[c:] * decay
            t_inv = _unit_lower_inverse(l_mat, block_diag, eye, inv_precision)
            uw = _dot(t_inv, jnp.concatenate([vb, kb * eg], axis=1), inv_precision)
            u = uw[:, :GDN_DV]
            w = uw[:, GDN_DV:]
            s_h = state[h]
            ws_qs = _dot(jnp.concatenate([w, qh * eg], axis=0).astype(BF16), s_h.astype(BF16))
            v_new = (u - ws_qs[:c]).astype(BF16)
            o = ws_qs[c:] + _dot(qk.astype(BF16), v_new)
            d_state = lax.dot_general((kh * et).astype(BF16), v_new, tn_dims,
                                      preferred_element_type=F32)
            state[h] = s_h * e_last[:, h:h + 1] + d_state
            zh = z_ref[rows, sl]
            o_ref[rows, sl] = (_rms(o, norm_w) * _silu(zh)).astype(o_ref.dtype)
        return carry

    lax.fori_loop(0, tm // c, chunk, 0)


def gdn(proj, proj_small, qkv_col, z_col, ba_col, conv_w, a_log, dt_bias, norm_w, *, tm, inv_precision):
    s = proj.shape[0]
    width = GDN_HEADS * GDN_DV
    full = lambda a: pl.BlockSpec(a.shape, lambda i: (0,) * a.ndim)
    return pl.pallas_call(
        functools.partial(_gdn_kernel, tm=tm, inv_precision=inv_precision),
        grid=(s // tm,),
        in_specs=[
            pl.BlockSpec((tm, GDN_QKV_COLS), lambda i: (i, qkv_col)),
            pl.BlockSpec((tm, width), lambda i: (i, z_col)),
            pl.BlockSpec((tm, LANES), lambda i: (i, ba_col)),
            full(conv_w), full(a_log), full(dt_bias), full(norm_w),
        ],
        out_specs=pl.BlockSpec((tm, width), lambda i: (i, 0)),
        out_shape=jax.ShapeDtypeStruct((s, width), BF16),
        scratch_shapes=[
            pltpu.VMEM((tm + SUBLANES, GDN_QKV_COLS), F32),
            pltpu.VMEM((tm, width), F32),
            pltpu.VMEM((tm, width), F32),
            pltpu.VMEM((tm, width), F32),
            pltpu.VMEM((tm, GDN_HEADS), F32),
            pltpu.VMEM((tm, GDN_HEADS), F32),
            pltpu.VMEM((GDN_HEADS, GDN_DK, GDN_DV), F32),
        ],
        compiler_params=_params("arbitrary"),
        name="gdn",
    )(proj, proj, proj_small, conv_w, a_log, dt_bias, norm_w)


def _merge_kernel(ym_ref, yl_ref, yg_ref, g0_ref, g1_ref, g2_ref, bg_ref, wb_ref, o_ref):
    bg = bg_ref[...]
    acc = None
    for n, (y_ref, g_ref) in enumerate(((ym_ref, g0_ref), (yl_ref, g1_ref), (yg_ref, g2_ref))):
        up = _dot(y_ref[...], wb_ref[n])
        term = jax.nn.sigmoid(g_ref[...] + bg[n:n + 1, :]) * up
        acc = term if acc is None else acc + term
    o_ref[...] = acc.astype(o_ref.dtype)


def merge(y_mla, y_lru, y_gdn, proj, b_gate, w_branch, *, tm, tn):
    s, bw = y_mla.shape
    d = w_branch.shape[2]
    nj = d // tn
    y_spec = pl.BlockSpec((tm, bw), lambda i, j: (i, 0))
    gate_spec = lambda n: pl.BlockSpec((tm, tn), lambda i, j: (i, n * nj + j))
    return pl.pallas_call(
        _merge_kernel,
        grid=(s // tm, nj),
        in_specs=[y_spec, y_spec, y_spec, gate_spec(0), gate_spec(1), gate_spec(2),
                  pl.BlockSpec((N_BRANCH, tn), lambda i, j: (0, j)),
                  pl.BlockSpec((N_BRANCH, bw, tn), lambda i, j: (0, 0, j))],
        out_specs=pl.BlockSpec((tm, tn), lambda i, j: (i, j)),
        out_shape=jax.ShapeDtypeStruct((s, d), BF16),
        compiler_params=_params("arbitrary", "arbitrary"),
        name="merge",
    )(y_mla, y_lru, y_gdn, proj, proj, proj, b_gate, w_branch)


def _residual_matmul_kernel(x_ref, a_ref, w_ref, o_ref):
    o_ref[...] = x_ref[...] + _dot(a_ref[...], w_ref[...])


def residual_matmul(x, a, w, *, tm, tn):
    s, k = a.shape
    n = w.shape[1]
    return pl.pallas_call(
        _residual_matmul_kernel,
        grid=(s // tm, n // tn),
        in_specs=[
            pl.BlockSpec((tm, tn), lambda i, j: (i, j)),
            pl.BlockSpec((tm, k), lambda i, j: (i, 0)),
            pl.BlockSpec((k, tn), lambda i, j: (0, j)),
        ],
        out_specs=pl.BlockSpec((tm, tn), lambda i, j: (i, j)),
        out_shape=jax.ShapeDtypeStruct((s, n), F32),
        compiler_params=_params("arbitrary", "arbitrary"),
        name="residual_matmul",
    )(x, a, w)


def _ffn_kernel(x_ref, g_ref, wg_ref, wu_ref, wd_ref, fg_ref, o_ref, h_ref, acc_ref, *, final_norm):
    j = pl.program_id(1)

    @pl.when(j == 0)
    def _():
        x = x_ref[...]
        h_ref[...] = _rms(x, g_ref[...]).astype(BF16)
        acc_ref[...] = x

    h = h_ref[...]
    a = _silu(_dot(h, wg_ref[...])) * _dot(h, wu_ref[...])
    acc_ref[...] += _dot(a.astype(BF16), wd_ref[...])

    @pl.when(j == pl.num_programs(1) - 1)
    def _():
        y = acc_ref[...]
        o_ref[...] = _rms(y, fg_ref[...]) if final_norm else y


def ffn(x, gain, w_gate, w_up, w_down, final_gain, *, tm, th, final_norm):
    s, d = x.shape
    hidden = w_gate.shape[1]
    return pl.pallas_call(
        functools.partial(_ffn_kernel, final_norm=final_norm),
        grid=(s // tm, hidden // th),
        in_specs=[
            pl.BlockSpec((tm, d), lambda i, j: (i, 0)),
            pl.BlockSpec((1, d), lambda i, j: (0, 0)),
            pl.BlockSpec((d, th), lambda i, j: (0, j)),
            pl.BlockSpec((d, th), lambda i, j: (0, j)),
            pl.BlockSpec((th, d), lambda i, j: (j, 0)),
            pl.BlockSpec((1, d), lambda i, j: (0, 0)),
        ],
        out_specs=pl.BlockSpec((tm, d), lambda i, j: (i, 0)),
        out_shape=jax.ShapeDtypeStruct((s, d), F32),
        scratch_shapes=[pltpu.VMEM((tm, d), BF16), pltpu.VMEM((tm, d), F32)],
        compiler_params=_params("arbitrary", "arbitrary"),
        name="ffn",
    )(x, gain, w_gate, w_up, w_down, final_gain)


def _split_w_in(w):
    sizes = [Q_LORA, KV_LORA, QK_ROPE, LRU_WIDTH, LRU_WIDTH,
             GDN_HEADS * GDN_DK, GDN_HEADS * GDN_DK, GDN_HEADS * GDN_DV, GDN_HEADS * GDN_DV,
             GDN_HEADS, GDN_HEADS, GATE_COLS]
    parts, off = [], 0
    for n in sizes:
        parts.append(w[:, off:off + n])
        off += n
    return parts


def _layer_weights(w_in, w_uq, w_ukv, lru_w_a, lru_w_x):
    (c_q, c_kv, k_rope, lru_x, lru_y, g_q, g_k, g_v, g_z, g_b, g_a, gates) = _split_w_in(w_in)
    d = w_in.shape[0]
    w_main = jnp.concatenate([gates, g_q, g_k, g_v, g_z, lru_x, lru_y], axis=1).astype(BF16)
    pad = jnp.zeros((d, SMALL_COLS - SMALL_A_OFF - GDN_HEADS), w_in.dtype)
    w_small = jnp.concatenate([c_q, c_kv, k_rope, g_b, g_a, pad], axis=1).astype(BF16)

    uq = w_uq.reshape(Q_LORA, MLA_HEADS, QK_DIM)
    wqn = uq[:, :, :QK_NOPE].reshape(Q_LORA, MLA_HEADS * QK_NOPE).astype(BF16)
    wq1 = uq[:, :, QK_NOPE:QK_NOPE + ROPE_HALF].reshape(Q_LORA, MLA_HEADS * ROPE_HALF).astype(BF16)
    wq2 = uq[:, :, QK_NOPE + ROPE_HALF:].reshape(Q_LORA, MLA_HEADS * ROPE_HALF).astype(BF16)
    ukv = w_ukv.reshape(KV_LORA, MLA_HEADS, QK_NOPE + V_HEAD)
    wk = ukv[:, :, :QK_NOPE].reshape(KV_LORA, MLA_HEADS * QK_NOPE).astype(BF16)
    wv = ukv[:, :, QK_NOPE:].reshape(KV_LORA, MLA_HEADS * V_HEAD).astype(BF16)

    def pair_tiles(w):
        z = jnp.zeros((LRU_BLOCK_W, LRU_BLOCK_W), w.dtype)
        return jnp.stack([jnp.block([[w[2 * p], z], [z, w[2 * p + 1]]]) for p in range(LRU_BLOCKS // 2)])

    w_gates = jnp.concatenate([pair_tiles(lru_w_a), pair_tiles(lru_w_x)], axis=2).astype(BF16)
    return w_main, w_small, wqn, wq1, wq2, wk, wv, w_gates


def _row(v):
    return v.reshape(1, -1)


def kernel(x, positions, norm_mix, w_in, mla_q_norm, mla_w_uq, mla_kv_norm, mla_w_ukv, lru_conv_w, lru_conv_b, lru_w_a, lru_b_a, lru_w_x, lru_b_x, lru_lambda, gdn_conv_w, gdn_a_log, gdn_dt_bias, gdn_norm, w_branch, b_gate, w_out, norm_ffn, ffn_w_gate, ffn_w_up, ffn_w_down, norm_final):
    b, s, d = x.shape
    assert b == 1
    depth = w_in.shape[0]
    xs = x.reshape(s, d)

    half = ROPE_HALF
    inv_freq = ROPE_THETA ** (-jnp.arange(half, dtype=F32) / half)
    inv_b = jnp.tile(inv_freq, MLA_HEADS).reshape(1, MLA_HEADS * half)
    pos_b = jnp.broadcast_to(positions.astype(F32).reshape(s, 1), (s, MLA_HEADS * half))
    cos, sin = rope_tables(pos_b, inv_b, tm=1024)

    main_cols = GATE_COLS + GDN_QKV_COLS + GDN_HEADS * GDN_DV + 2 * LRU_WIDTH
    qkv_col = GATE_COLS // GDN_QKV_COLS
    z_col = (GATE_COLS + GDN_QKV_COLS) // (GDN_HEADS * GDN_DV)
    lru_x_col = (main_cols - 2 * LRU_WIDTH) // LRU_WIDTH
    ba_col = SMALL_ROPE_OFF // LANES

    for l in range(depth):
        w_main, w_small, wqn, wq1, wq2, wk, wv, w_gates = _layer_weights(
            w_in[l], mla_w_uq[l], mla_w_ukv[l], lru_w_a[l], lru_w_x[l])
        gain = _row(norm_mix[l])
        proj = norm_matmul(xs, gain, w_main, tm=1024, tn=1024)
        proj_small = norm_matmul(xs, gain, w_small, tm=1024, tn=SMALL_COLS)

        q, k, v = mla_prep(proj_small, _row(mla_q_norm[l]), _row(mla_kv_norm[l]),
                           wqn, wq1, wq2, wk, wv, cos, sin, tm=512)
        y_mla = attention(q, k, v, tq=512)
        y_lru = rglru(proj, lru_x_col, lru_x_col + 1, lru_conv_w[l], _row(lru_conv_b[l]), w_gates,
                      _row(lru_b_a[l]), _row(lru_b_x[l]), _row(lru_lambda[l]), tm=512)
        y_gdn = gdn(proj, proj_small, qkv_col, z_col, ba_col, gdn_conv_w[l], _row(gdn_a_log[l]),
                    _row(gdn_dt_bias[l]), _row(gdn_norm[l]), tm=512, inv_precision=HIGHEST)

        mixed = merge(y_mla, y_lru, y_gdn, proj, b_gate[l], w_branch[l].astype(BF16), tm=512, tn=512)
        xs = residual_matmul(xs, mixed, w_out[l].astype(BF16), tm=1024, tn=1024)
        xs = ffn(xs, _row(norm_ffn[l]), ffn_w_gate[l].astype(BF16), ffn_w_up[l].astype(BF16),
                 ffn_w_down[l].astype(BF16), _row(norm_final), tm=512, th=512,
                 final_norm=(l == depth - 1))
    return xs.reshape(b, s, d)
```

```python
import functools
import math

import jax
import jax.numpy as jnp
from jax import lax
from jax.experimental import pallas as pl
from jax.experimental.pallas import tpu as pltpu

F32 = jnp.float32
BF16 = jnp.bfloat16
HIGHEST = lax.Precision.HIGHEST

EPS = 1e-6
N_BRANCH = 3
MLA_HEADS = 8
QK_NOPE = 128
QK_ROPE = 64
ROPE_HALF = QK_ROPE // 2
QK_DIM = QK_NOPE + QK_ROPE
V_HEAD = 128
Q_LORA = 512
KV_LORA = 512
ROPE_THETA = 10000.0
LRU_WIDTH = 1024
LRU_BLOCKS = 8
LRU_BLOCK_W = LRU_WIDTH // LRU_BLOCKS
LRU_C = 8.0
CONV_W = 4
GDN_HEADS = 8
GDN_DK = 128
GDN_DV = 128
GDN_CHUNK = 64
GDN_INV_BLOCK = 16

V7X_VMEM_BYTES = 64 * 1024 * 1024
VMEM_LIMIT_BYTES = V7X_VMEM_BYTES - 8 * 1024 * 1024
SUBLANES = 8
LANES = 128

GATE_COLS = N_BRANCH * 2048
GDN_QKV_COLS = 3 * GDN_HEADS * GDN_DK
SMALL_COLS = 1152
SMALL_ROPE_OFF = Q_LORA + KV_LORA
SMALL_B_OFF = SMALL_ROPE_OFF + QK_ROPE
SMALL_A_OFF = SMALL_B_OFF + GDN_HEADS


def _params(*semantics):
    return pltpu.CompilerParams(dimension_semantics=semantics, vmem_limit_bytes=VMEM_LIMIT_BYTES)


def _dot(a, b, precision=None):
    return jnp.dot(a, b, preferred_element_type=F32, precision=precision)


def _rms(x, gain):
    return x * lax.rsqrt(jnp.mean(x * x, axis=-1, keepdims=True) + EPS) * gain


def _softplus(x):
    return jnp.maximum(x, 0.0) + jnp.log1p(jnp.exp(-jnp.abs(x)))


def _silu(x):
    return x * jax.nn.sigmoid(x)


def _gelu_tanh(x):
    return 0.5 * x * (1.0 + jnp.tanh(math.sqrt(2.0 / math.pi) * (x + 0.044715 * (x * x * x))))


def _norm_matmul_kernel(x_ref, g_ref, w_ref, o_ref, h_ref):
    @pl.when(pl.program_id(1) == 0)
    def _():
        h_ref[...] = _rms(x_ref[...], g_ref[...]).astype(BF16)

    o_ref[...] = _dot(h_ref[...], w_ref[...]).astype(o_ref.dtype)


def _per_layer(a, layer):
    return pl.BlockSpec((None,) + a.shape[1:], lambda *_: (layer,) + (0,) * (a.ndim - 1))


def norm_matmul(x, gain, w, layer, *, tm, tn):
    s, k = x.shape
    n = w.shape[2]
    return pl.pallas_call(
        _norm_matmul_kernel,
        grid=(s // tm, n // tn),
        in_specs=[
            pl.BlockSpec((tm, k), lambda i, j: (i, 0)),
            _per_layer(gain, layer),
            pl.BlockSpec((None, k, tn), lambda i, j: (layer, 0, j)),
        ],
        out_specs=pl.BlockSpec((tm, tn), lambda i, j: (i, j)),
        out_shape=jax.ShapeDtypeStruct((s, n), F32),
        scratch_shapes=[pltpu.VMEM((tm, k), BF16)],
        compiler_params=_params("arbitrary", "arbitrary"),
        name="norm_matmul",
    )(x, gain, w)


def _rope_table_kernel(pos_ref, inv_ref, cos_ref, sin_ref):
    ang = pos_ref[...] * inv_ref[...]
    cos_ref[...] = jnp.cos(ang)
    sin_ref[...] = jnp.sin(ang)


def rope_tables(pos_b, inv_b, *, tm):
    s, n = pos_b.shape
    blk = pl.BlockSpec((tm, n), lambda i: (i, 0))
    return pl.pallas_call(
        _rope_table_kernel,
        grid=(s // tm,),
        in_specs=[blk, pl.BlockSpec((1, n), lambda i: (0, 0))],
        out_specs=[blk, blk],
        out_shape=[jax.ShapeDtypeStruct((s, n), F32)] * 2,
        compiler_params=_params("arbitrary"),
        name="rope_tables",
    )(pos_b, inv_b)


def _mla_prep_kernel(p_ref, qg_ref, kvg_ref, wqn_ref, wq1_ref, wq2_ref, wk_ref, wv_ref,
                     cos_ref, sin_ref, q_out, k_out, v_out):
    p = p_ref[...]
    hq = _rms(p[:, :Q_LORA], qg_ref[...]).astype(BF16)
    hkv = _rms(p[:, Q_LORA:SMALL_ROPE_OFF], kvg_ref[...]).astype(BF16)
    kr = p[:, SMALL_ROPE_OFF:SMALL_ROPE_OFF + QK_ROPE]
    cos = cos_ref[...]
    sin = sin_ref[...]
    qn = _dot(hq, wqn_ref[...])
    q1 = _dot(hq, wq1_ref[...])
    q2 = _dot(hq, wq2_ref[...])
    qr1 = q1 * cos - q2 * sin
    qr2 = q2 * cos + q1 * sin
    kn = _dot(hkv, wk_ref[...])
    v = _dot(hkv, wv_ref[...])
    c32 = cos[:, :ROPE_HALF]
    s32 = sin[:, :ROPE_HALF]
    k1 = kr[:, :ROPE_HALF]
    k2 = kr[:, ROPE_HALF:]
    kr1 = k1 * c32 - k2 * s32
    kr2 = k2 * c32 + k1 * s32
    for h in range(MLA_HEADS):
        nope = slice(h * QK_NOPE, (h + 1) * QK_NOPE)
        rope = slice(h * ROPE_HALF, (h + 1) * ROPE_HALF)
        q_out[h] = jnp.concatenate([qn[:, nope], qr1[:, rope], qr2[:, rope]], axis=-1).astype(BF16)
        k_out[h] = jnp.concatenate([kn[:, nope], kr1, kr2], axis=-1).astype(BF16)
        v_out[h] = v[:, h * V_HEAD:(h + 1) * V_HEAD].astype(BF16)


def mla_prep(proj_small, q_gain, kv_gain, wqn, wq1, wq2, wk, wv, cos, sin, layer, *, tm):
    s = proj_small.shape[0]
    full = lambda a: _per_layer(a, layer)
    rows = lambda n: pl.BlockSpec((tm, n), lambda i: (i, 0))
    heads = lambda d: pl.BlockSpec((MLA_HEADS, tm, d), lambda i: (0, i, 0))
    return pl.pallas_call(
        _mla_prep_kernel,
        grid=(s // tm,),
        in_specs=[rows(SMALL_COLS), full(q_gain), full(kv_gain), full(wqn), full(wq1), full(wq2),
                  full(wk), full(wv), rows(cos.shape[1]), rows(sin.shape[1])],
        out_specs=[heads(QK_DIM), heads(QK_DIM), heads(V_HEAD)],
        out_shape=[jax.ShapeDtypeStruct((MLA_HEADS, s, QK_DIM), BF16),
                   jax.ShapeDtypeStruct((MLA_HEADS, s, QK_DIM), BF16),
                   jax.ShapeDtypeStruct((MLA_HEADS, s, V_HEAD), BF16)],
        compiler_params=_params("arbitrary"),
        name="mla_prep",
    )(proj_small, q_gain, kv_gain, wqn, wq1, wq2, wk, wv, cos, sin)


def _attention_kernel(q_ref, k_ref, v_ref, o_ref, *, tk, n_sub, exp2_scale):
    i = pl.program_id(1)
    qs = [q_ref[0, a * tk:(a + 1) * tk, :] for a in range(n_sub)]

    def kv_block(j):
        start = pl.multiple_of(j * tk, tk)
        return k_ref[0, pl.ds(start, tk), :], v_ref[0, pl.ds(start, tk), :]

    def update(q, kv, carry, diagonal):
        m, l, acc = carry
        kj, vj = kv
        s = lax.dot_general(q, kj, (((1,), (1,)), ((), ())), preferred_element_type=F32)
        if diagonal:
            row = lax.broadcasted_iota(jnp.int32, (tk, tk), 0)
            col = lax.broadcasted_iota(jnp.int32, (tk, tk), 1)
            s = jnp.where(col <= row, s, -jnp.inf)
        m_new = jnp.maximum(m, jnp.max(s, axis=-1, keepdims=True))
        p = jnp.exp2((s - m_new) * exp2_scale)
        alpha = jnp.exp2((m - m_new) * exp2_scale)
        l = alpha * l + jnp.sum(p, axis=-1, keepdims=True)
        acc = alpha * acc + _dot(p.astype(BF16), vj)
        return m_new, l, acc

    def full_blocks(j, carries):
        kv = kv_block(j)
        return tuple(update(qs[a], kv, carries[a], False) for a in range(n_sub))

    init = (jnp.full((tk, 1), -jnp.inf, F32), jnp.zeros((tk, 1), F32), jnp.zeros((tk, V_HEAD), F32))
    carries = list(lax.fori_loop(0, i * n_sub, full_blocks, (init,) * n_sub))
    for b in range(n_sub):
        kv = kv_block(i * n_sub + b)
        for a in range(b, n_sub):
            carries[a] = update(qs[a], kv, carries[a], a == b)
    for a in range(n_sub):
        _, l, acc = carries[a]
        o_ref[a * tk:(a + 1) * tk, :] = (acc / l).astype(o_ref.dtype)


def attention(q, k, v, *, tk, n_sub):
    h, s, _ = q.shape
    tq = tk * n_sub
    exp2_scale = QK_DIM ** -0.5 * math.log2(math.e)
    return pl.pallas_call(
        functools.partial(_attention_kernel, tk=tk, n_sub=n_sub, exp2_scale=exp2_scale),
        grid=(h, s // tq),
        in_specs=[
            pl.BlockSpec((1, tq, QK_DIM), lambda hh, i: (hh, i, 0)),
            pl.BlockSpec((1, s, QK_DIM), lambda hh, i: (hh, 0, 0)),
            pl.BlockSpec((1, s, V_HEAD), lambda hh, i: (hh, 0, 0)),
        ],
        out_specs=pl.BlockSpec((tq, V_HEAD), lambda hh, i: (i, hh)),
        out_shape=jax.ShapeDtypeStruct((s, h * V_HEAD), BF16),
        compiler_params=_params("arbitrary", "arbitrary"),
        name="attention",
    )(q, k, v)


def _causal_conv_tile(x_ref, cw_ref, xbuf, tm):
    @pl.when(pl.program_id(0) == 0)
    def _():
        xbuf[0:SUBLANES, :] = jnp.zeros((SUBLANES, xbuf.shape[1]), F32)

    xbuf[SUBLANES:SUBLANES + tm, :] = x_ref[...]
    cw = cw_ref[...]
    acc = cw[0:1, :] * xbuf[pl.ds(SUBLANES - CONV_W + 1, tm), :]
    for k in range(1, CONV_W):
        acc = acc + cw[k:k + 1, :] * xbuf[pl.ds(SUBLANES - CONV_W + 1 + k, tm), :]
    xbuf[0:SUBLANES, :] = xbuf[tm:tm + SUBLANES, :]
    return acc


def _rglru_kernel(x_ref, y_ref, cw_ref, cb_ref, wg_ref, ba_ref, bx_ref, lam_ref, o_ref,
                  xbuf, a_s, b_s, h_carry, *, tm):
    @pl.when(pl.program_id(0) == 0)
    def _():
        h_carry[...] = jnp.zeros(h_carry.shape, F32)

    xc = _causal_conv_tile(x_ref, cw_ref, xbuf, tm) + cb_ref[...]
    xcb = xc.astype(BF16)
    pair_w = 2 * LRU_BLOCK_W
    r_parts, i_parts = [], []
    for p in range(LRU_BLOCKS // 2):
        g = _dot(xcb[:, p * pair_w:(p + 1) * pair_w], wg_ref[p])
        r_parts.append(g[:, :pair_w])
        i_parts.append(g[:, pair_w:])
    r = jax.nn.sigmoid(jnp.concatenate(r_parts, axis=-1) + ba_ref[...])
    gi = jax.nn.sigmoid(jnp.concatenate(i_parts, axis=-1) + bx_ref[...])
    log_a = -LRU_C * r * _softplus(-lam_ref[...])
    t = jnp.tanh(log_a)
    mult = jnp.sqrt(-2.0 * t / (1.0 - t))
    a_s[...] = jnp.exp(log_a)
    b_s[...] = mult * (gi * xc)

    row = lax.broadcasted_iota(jnp.int32, (SUBLANES, LRU_WIDTH), 0)

    def group(gidx, h_prev):
        r0 = pl.multiple_of(gidx * SUBLANES, SUBLANES)
        a = a_s[pl.ds(r0, SUBLANES), :]
        b = b_s[pl.ds(r0, SUBLANES), :]
        shift = 1
        while shift < SUBLANES:
            keep = row >= shift
            a_sh = jnp.where(keep, pltpu.roll(a, shift, 0), 1.0)
            b_sh = jnp.where(keep, pltpu.roll(b, shift, 0), 0.0)
            b = a * b_sh + b
            a = a * a_sh
            shift *= 2
        hs = a * h_prev + b
        b_s[pl.ds(r0, SUBLANES), :] = hs
        return hs[SUBLANES - 1:SUBLANES, :]

    h_carry[...] = lax.fori_loop(0, tm // SUBLANES, group, h_carry[...])
    o_ref[...] = (b_s[...] * _gelu_tanh(y_ref[...])).astype(o_ref.dtype)


def rglru(proj, x_col, y_col, conv_w, conv_b, w_gates, b_a, b_x, lam, layer, *, tm):
    s = proj.shape[0]
    full = lambda a: _per_layer(a, layer)
    return pl.pallas_call(
        functools.partial(_rglru_kernel, tm=tm),
        grid=(s // tm,),
        in_specs=[
            pl.BlockSpec((tm, LRU_WIDTH), lambda i: (i, x_col)),
            pl.BlockSpec((tm, LRU_WIDTH), lambda i: (i, y_col)),
            full(conv_w), full(conv_b), full(w_gates), full(b_a), full(b_x), full(lam),
        ],
        out_specs=pl.BlockSpec((tm, LRU_WIDTH), lambda i: (i, 0)),
        out_shape=jax.ShapeDtypeStruct((s, LRU_WIDTH), BF16),
        scratch_shapes=[
            pltpu.VMEM((tm + SUBLANES, LRU_WIDTH), F32),
            pltpu.VMEM((tm, LRU_WIDTH), F32),
            pltpu.VMEM((tm, LRU_WIDTH), F32),
            pltpu.VMEM((1, LRU_WIDTH), F32),
        ],
        compiler_params=_params("arbitrary"),
        name="rglru",
    )(proj, proj, conv_w, conv_b, w_gates, b_a, b_x, lam)


def _unit_lower_solve(l_mats, rhs, block_diag, cast, mm):
    heads = range(len(l_mats))
    d = [jnp.where(block_diag, l_mats[h], 0.0) for h in heads]
    n = [l_mats[h] - d[h] for h in heads]
    dc = [cast(d[h]) for h in heads]
    d2 = [mm(dc[h], dc[h]) for h in heads]
    d2c = [cast(d2[h]) for h in heads]
    d4 = [mm(d2c[h], d2c[h]) for h in heads]
    d4c = [cast(d4[h]) for h in heads]
    d8 = [mm(d4c[h], d4c[h]) for h in heads]
    d8c = [cast(d8[h]) for h in heads]
    x = [-d[h] for h in heads]
    for dp, dpc in ((d2, d2c), (d4, d4c), (d8, d8c)):
        x = [x[h] + dp[h] + mm(cast(x[h]), dpc[h]) for h in heads]
    xc = [cast(x[h]) for h in heads]
    m = [n[h] + mm(xc[h], cast(n[h])) for h in heads]
    mc = [cast(m[h]) for h in heads]
    m2 = [mm(mc[h], mc[h]) for h in heads]
    y = [m2[h] - m[h] - mm(mc[h], cast(m2[h])) for h in heads]
    r1 = [rhs[h] + mm(xc[h], cast(rhs[h])) for h in heads]
    return [r1[h] + mm(cast(y[h]), cast(r1[h])) for h in heads]


def _gdn_kernel(qkv_ref, z_ref, ba_ref, cw_ref, alog_ref, dtb_ref, nw_ref, o_ref,
                xbuf, q_s, k_s, v_s, g_s, beta_s, u_s, wq_s, kt_s, qk_s, el_s, state,
                *, tm, exact_inverse):
    @pl.when(pl.program_id(0) == 0)
    def _():
        state[...] = jnp.zeros(state.shape, F32)

    act = _silu(_causal_conv_tile(qkv_ref, cw_ref, xbuf, tm))
    kw = GDN_HEADS * GDN_DK
    for h in range(GDN_HEADS):
        sl = slice(h * GDN_DK, (h + 1) * GDN_DK)
        qh = act[:, h * GDN_DK:(h + 1) * GDN_DK]
        kh = act[:, kw + h * GDN_DK:kw + (h + 1) * GDN_DK]
        q_s[:, sl] = qh * lax.rsqrt(jnp.sum(qh * qh, axis=-1, keepdims=True) + EPS) * (GDN_DK ** -0.5)
        k_s[:, sl] = kh * lax.rsqrt(jnp.sum(kh * kh, axis=-1, keepdims=True) + EPS)
    v_s[...] = act[:, 2 * kw:]
    ba = ba_ref[...]
    beta_s[...] = jax.nn.sigmoid(ba[:, SMALL_B_OFF % LANES:SMALL_B_OFF % LANES + GDN_HEADS])
    a_logit = ba[:, SMALL_A_OFF % LANES:SMALL_A_OFF % LANES + GDN_HEADS]
    g_s[...] = -jnp.exp(alog_ref[...]) * _softplus(a_logit + dtb_ref[...])

    c = GDN_CHUNK
    row = lax.broadcasted_iota(jnp.int32, (c, c), 0)
    col = lax.broadcasted_iota(jnp.int32, (c, c), 1)
    incl = row >= col
    strict = row > col
    tril = incl.astype(F32)
    stril = strict.astype(F32)
    block_diag = jnp.logical_and(strict, (row // GDN_INV_BLOCK) == (col // GDN_INV_BLOCK))
    norm_w = nw_ref[...]
    nt_dims = (((1,), (1,)), ((), ()))
    tn_dims = (((0,), (0,)), ((), ()))
    heads = range(GDN_HEADS)
    hs = lambda h: slice(h * GDN_DK, (h + 1) * GDN_DK)
    cs = lambda h: slice(h * c, (h + 1) * c)
    bf = lambda t: t.astype(BF16)
    if exact_inverse:
        cast = lambda t: t
        mm = lambda a, b: _dot(a, b, HIGHEST)
    else:
        cast = bf
        mm = _dot

    def prepare(ci, carry):
        r0 = pl.multiple_of(ci * c, c)
        rows = pl.ds(r0, c)
        g_c = g_s[rows, :]
        beta_c = beta_s[rows, :]
        gc = _dot(tril, g_c, HIGHEST)
        gb = jnp.concatenate([jnp.broadcast_to(g_c[:, h:h + 1], (c, c)) * stril for h in heads], axis=1)
        diff = _dot(tril, gb, HIGHEST)
        e_gc = jnp.exp(gc)
        g_last = gc[c - 1:c, :]
        e_tail = jnp.exp(g_last - gc)
        el_s[pl.ds(ci, 1), :] = jnp.exp(g_last)
        q = [q_s[rows, hs(h)] for h in heads]
        k = [k_s[rows, hs(h)] for h in heads]
        bh = [jnp.broadcast_to(beta_c[:, h:h + 1], (c, GDN_DK)) for h in heads]
        eg = [jnp.broadcast_to(e_gc[:, h:h + 1], (c, GDN_DK)) for h in heads]
        kb = [k[h] * bh[h] for h in heads]
        decay = [jnp.where(incl, jnp.exp(diff[:, cs(h)]), 0.0) for h in heads]
        kq = [lax.dot_general(bf(jnp.concatenate([kb[h], q[h]], axis=0)), bf(k[h]), nt_dims,
                              preferred_element_type=F32) for h in heads]
        l_mats = [jnp.where(strict, kq[h][:c] * decay[h], 0.0) for h in heads]
        for h in heads:
            qk_s[rows, cs(h)] = bf(kq[h][c:] * decay[h])
            kt_s[rows, hs(h)] = bf(k[h] * jnp.broadcast_to(e_tail[:, h:h + 1], (c, GDN_DK)))
        rhs = [jnp.concatenate([v_s[rows, hs(h)] * bh[h], kb[h] * eg[h]], axis=1) for h in heads]
        uw = _unit_lower_solve(l_mats, rhs, block_diag, cast, mm)
        rows_w = pl.ds(pl.multiple_of(2 * r0, 2 * c), c)
        rows_q = pl.ds(pl.multiple_of(2 * r0 + c, c), c)
        for h in heads:
            u_s[rows, hs(h)] = uw[h][:, :GDN_DV]
            wq_s[rows_w, hs(h)] = bf(uw[h][:, GDN_DV:])
            wq_s[rows_q, hs(h)] = bf(q[h] * eg[h])
        return carry

    lax.fori_loop(0, tm // c, prepare, 0)

    def advance(ci, carry):
        r0 = pl.multiple_of(ci * c, c)
        rows = pl.ds(r0, c)
        rows_wq = pl.ds(pl.multiple_of(2 * r0, 2 * c), 2 * c)
        e_last = el_s[pl.ds(ci, 1), :]
        s = [state[h] for h in heads]
        ws_qs = [_dot(wq_s[rows_wq, hs(h)], bf(s[h])) for h in heads]
        v_new = [bf(u_s[rows, hs(h)] - ws_qs[h][:c]) for h in heads]
        o = [ws_qs[h][c:] + _dot(qk_s[rows, cs(h)], v_new[h]) for h in heads]
        d_state = [lax.dot_general(kt_s[rows, hs(h)], v_new[h], tn_dims, preferred_element_type=F32)
                   for h in heads]
        for h in heads:
            state[h] = s[h] * e_last[:, h:h + 1] + d_state[h]
        for h in heads:
            o_ref[rows, hs(h)] = (_rms(o[h], norm_w) * _silu(z_ref[rows, hs(h)])).astype(o_ref.dtype)
        return carry

    lax.fori_loop(0, tm // c, advance, 0)


def gdn(proj, proj_small, qkv_col, z_col, ba_col, conv_w, a_log, dt_bias, norm_w, layer, *, tm,
        exact_inverse):
    s = proj.shape[0]
    width = GDN_HEADS * GDN_DV
    per_layer = lambda a: pl.BlockSpec((None,) + a.shape[1:], lambda i: (layer,) + (0,) * (a.ndim - 1))
    return pl.pallas_call(
        functools.partial(_gdn_kernel, tm=tm, exact_inverse=exact_inverse),
        grid=(s // tm,),
        in_specs=[
            pl.BlockSpec((tm, GDN_QKV_COLS), lambda i: (i, qkv_col)),
            pl.BlockSpec((tm, width), lambda i: (i, z_col)),
            pl.BlockSpec((tm, LANES), lambda i: (i, ba_col)),
            per_layer(conv_w), per_layer(a_log), per_layer(dt_bias), per_layer(norm_w),
        ],
        out_specs=pl.BlockSpec((tm, width), lambda i: (i, 0)),
        out_shape=jax.ShapeDtypeStruct((s, width), BF16),
        scratch_shapes=[
            pltpu.VMEM((tm + SUBLANES, GDN_QKV_COLS), F32),
            pltpu.VMEM((tm, width), F32),
            pltpu.VMEM((tm, width), F32),
            pltpu.VMEM((tm, width), F32),
            pltpu.VMEM((tm, GDN_HEADS), F32),
            pltpu.VMEM((tm, GDN_HEADS), F32),
            pltpu.VMEM((tm, width), F32),
            pltpu.VMEM((2 * tm, width), BF16),
            pltpu.VMEM((tm, width), BF16),
            pltpu.VMEM((tm, GDN_HEADS * GDN_CHUNK), BF16),
            pltpu.VMEM((tm // GDN_CHUNK, GDN_HEADS), F32),
            pltpu.VMEM((GDN_HEADS, GDN_DK, GDN_DV), F32),
        ],
        compiler_params=_params("arbitrary"),
        name="gdn",
    )(proj, proj, proj_small, conv_w, a_log, dt_bias, norm_w)


def _merge_kernel(ym_ref, yl_ref, yg_ref, g0_ref, g1_ref, g2_ref, bg_ref, wb_ref, o_ref):
    bg = bg_ref[...]
    acc = None
    for n, (y_ref, g_ref) in enumerate(((ym_ref, g0_ref), (yl_ref, g1_ref), (yg_ref, g2_ref))):
        up = _dot(y_ref[...], wb_ref[n])
        term = jax.nn.sigmoid(g_ref[...] + bg[n:n + 1, :]) * up
        acc = term if acc is None else acc + term
    o_ref[...] = acc.astype(o_ref.dtype)


def merge(y_mla, y_lru, y_gdn, proj, b_gate, w_branch, layer, *, tm, tn):
    s, bw = y_mla.shape
    d = w_branch.shape[3]
    nj = d // tn
    y_spec = pl.BlockSpec((tm, bw), lambda i, j: (i, 0))
    gate_spec = lambda n: pl.BlockSpec((tm, tn), lambda i, j: (i, n * nj + j))
    return pl.pallas_call(
        _merge_kernel,
        grid=(s // tm, nj),
        in_specs=[y_spec, y_spec, y_spec, gate_spec(0), gate_spec(1), gate_spec(2),
                  pl.BlockSpec((None, N_BRANCH, tn), lambda i, j: (layer, 0, j)),
                  pl.BlockSpec((None, N_BRANCH, bw, tn), lambda i, j: (layer, 0, 0, j))],
        out_specs=pl.BlockSpec((tm, tn), lambda i, j: (i, j)),
        out_shape=jax.ShapeDtypeStruct((s, d), BF16),
        compiler_params=_params("arbitrary", "arbitrary"),
        name="merge",
    )(y_mla, y_lru, y_gdn, proj, proj, proj, b_gate, w_branch)


def _residual_matmul_kernel(x_ref, a_ref, w_ref, o_ref):
    o_ref[...] = x_ref[...] + _dot(a_ref[...], w_ref[...])


def residual_matmul(x, a, w, layer, *, tm, tn):
    s, k = a.shape
    n = w.shape[2]
    return pl.pallas_call(
        _residual_matmul_kernel,
        grid=(s // tm, n // tn),
        in_specs=[
            pl.BlockSpec((tm, tn), lambda i, j: (i, j)),
            pl.BlockSpec((tm, k), lambda i, j: (i, 0)),
            pl.BlockSpec((None, k, tn), lambda i, j: (layer, 0, j)),
        ],
        out_specs=pl.BlockSpec((tm, tn), lambda i, j: (i, j)),
        out_shape=jax.ShapeDtypeStruct((s, n), F32),
        compiler_params=_params("arbitrary", "arbitrary"),
        name="residual_matmul",
    )(x, a, w)


def _ffn_kernel(x_ref, g_ref, wg_ref, wu_ref, wd_ref, fg_ref, o_ref, h_ref, acc_ref, *, final_norm):
    j = pl.program_id(1)

    @pl.when(j == 0)
    def _():
        x = x_ref[...]
        h_ref[...] = _rms(x, g_ref[...]).astype(BF16)
        acc_ref[...] = x

    h = h_ref[...]
    a = _silu(_dot(h, wg_ref[...])) * _dot(h, wu_ref[...])
    acc_ref[...] += _dot(a.astype(BF16), wd_ref[...])

    @pl.when(j == pl.num_programs(1) - 1)
    def _():
        y = acc_ref[...]
        o_ref[...] = _rms(y, fg_ref[...]) if final_norm else y


def ffn(x, gain, w_gate, w_up, w_down, final_gain, layer, *, tm, th, final_norm):
    s, d = x.shape
    hidden = w_gate.shape[2]
    return pl.pallas_call(
        functools.partial(_ffn_kernel, final_norm=final_norm),
        grid=(s // tm, hidden // th),
        in_specs=[
            pl.BlockSpec((tm, d), lambda i, j: (i, 0)),
            _per_layer(gain, layer),
            pl.BlockSpec((None, d, th), lambda i, j: (layer, 0, j)),
            pl.BlockSpec((None, d, th), lambda i, j: (layer, 0, j)),
            pl.BlockSpec((None, th, d), lambda i, j: (layer, j, 0)),
            pl.BlockSpec((1, d), lambda i, j: (0, 0)),
        ],
        out_specs=pl.BlockSpec((tm, d), lambda i, j: (i, 0)),
        out_shape=jax.ShapeDtypeStruct((s, d), F32),
        scratch_shapes=[pltpu.VMEM((tm, d), BF16), pltpu.VMEM((tm, d), F32)],
        compiler_params=_params("arbitrary", "arbitrary"),
        name="ffn",
    )(x, gain, w_gate, w_up, w_down, final_gain)


def _split_w_in(w):
    sizes = [Q_LORA, KV_LORA, QK_ROPE, LRU_WIDTH, LRU_WIDTH,
             GDN_HEADS * GDN_DK, GDN_HEADS * GDN_DK, GDN_HEADS * GDN_DV, GDN_HEADS * GDN_DV,
             GDN_HEADS, GDN_HEADS, GATE_COLS]
    parts, off = [], 0
    for n in sizes:
        parts.append(w[:, :, off:off + n])
        off += n
    return parts


def _regroup_weights(w_in, w_uq, w_ukv, lru_w_a, lru_w_x):
    (c_q, c_kv, k_rope, lru_x, lru_y, g_q, g_k, g_v, g_z, g_b, g_a, gates) = _split_w_in(w_in)
    depth, d, _ = w_in.shape
    w_main = jnp.concatenate([gates, g_q, g_k, g_v, g_z, lru_x, lru_y], axis=2).astype(BF16)
    pad = jnp.zeros((depth, d, SMALL_COLS - SMALL_A_OFF - GDN_HEADS), w_in.dtype)
    w_small = jnp.concatenate([c_q, c_kv, k_rope, g_b, g_a, pad], axis=2).astype(BF16)

    uq = w_uq.reshape(depth, Q_LORA, MLA_HEADS, QK_DIM)
    flat = lambda t: t.reshape(depth, t.shape[1], -1).astype(BF16)
    wqn = flat(uq[..., :QK_NOPE])
    wq1 = flat(uq[..., QK_NOPE:QK_NOPE + ROPE_HALF])
    wq2 = flat(uq[..., QK_NOPE + ROPE_HALF:])
    ukv = w_ukv.reshape(depth, KV_LORA, MLA_HEADS, QK_NOPE + V_HEAD)
    wk = flat(ukv[..., :QK_NOPE])
    wv = flat(ukv[..., QK_NOPE:])

    def pair_tiles(w):
        pairs = w.reshape(depth, LRU_BLOCKS // 2, 2, LRU_BLOCK_W, LRU_BLOCK_W)
        z = jnp.zeros_like(pairs[:, :, 0])
        top = jnp.concatenate([pairs[:, :, 0], z], axis=-1)
        bottom = jnp.concatenate([z, pairs[:, :, 1]], axis=-1)
        return jnp.concatenate([top, bottom], axis=-2)

    w_gates = jnp.concatenate([pair_tiles(lru_w_a), pair_tiles(lru_w_x)], axis=-1).astype(BF16)
    return w_main, w_small, wqn, wq1, wq2, wk, wv, w_gates


def _rows(v):
    return v.reshape(v.shape[0], 1, -1)


def kernel(x, positions, norm_mix, w_in, mla_q_norm, mla_w_uq, mla_kv_norm, mla_w_ukv, lru_conv_w, lru_conv_b, lru_w_a, lru_b_a, lru_w_x, lru_b_x, lru_lambda, gdn_conv_w, gdn_a_log, gdn_dt_bias, gdn_norm, w_branch, b_gate, w_out, norm_ffn, ffn_w_gate, ffn_w_up, ffn_w_down, norm_final):
    b, s, d = x.shape
    assert b == 1
    depth = w_in.shape[0]
    xs = x.reshape(s, d)

    half = ROPE_HALF
    inv_freq = ROPE_THETA ** (-jnp.arange(half, dtype=F32) / half)
    inv_b = jnp.tile(inv_freq, MLA_HEADS).reshape(1, MLA_HEADS * half)
    pos_b = jnp.broadcast_to(positions.astype(F32).reshape(s, 1), (s, MLA_HEADS * half))
    cos, sin = rope_tables(pos_b, inv_b, tm=1024)

    main_cols = GATE_COLS + GDN_QKV_COLS + GDN_HEADS * GDN_DV + 2 * LRU_WIDTH
    qkv_col = GATE_COLS // GDN_QKV_COLS
    z_col = (GATE_COLS + GDN_QKV_COLS) // (GDN_HEADS * GDN_DV)
    lru_x_col = (main_cols - 2 * LRU_WIDTH) // LRU_WIDTH
    ba_col = SMALL_ROPE_OFF // LANES

    w_main, w_small, wqn, wq1, wq2, wk, wv, w_gates = _regroup_weights(
        w_in, mla_w_uq, mla_w_ukv, lru_w_a, lru_w_x)
    w_branch_b = w_branch.astype(BF16)
    w_out_b = w_out.astype(BF16)
    ffn_gate_b = ffn_w_gate.astype(BF16)
    ffn_up_b = ffn_w_up.astype(BF16)
    ffn_down_b = ffn_w_down.astype(BF16)
    gain_mix, gain_ffn = _rows(norm_mix), _rows(norm_ffn)
    gain_q, gain_kv = _rows(mla_q_norm), _rows(mla_kv_norm)
    lru_cb, lru_ba, lru_bx, lru_lam = _rows(lru_conv_b), _rows(lru_b_a), _rows(lru_b_x), _rows(lru_lambda)
    gdn_alog, gdn_dtb, gdn_nw = _rows(gdn_a_log), _rows(gdn_dt_bias), _rows(gdn_norm)
    gain_final = norm_final.reshape(1, d)

    for l in range(depth):
        proj = norm_matmul(xs, gain_mix, w_main, l, tm=1024, tn=1024)
        proj_small = norm_matmul(xs, gain_mix, w_small, l, tm=1024, tn=SMALL_COLS)

        q, k, v = mla_prep(proj_small, gain_q, gain_kv, wqn, wq1, wq2, wk, wv, cos, sin, l, tm=512)
        y_mla = attention(q, k, v, tk=512, n_sub=2)
        y_lru = rglru(proj, lru_x_col, lru_x_col + 1, lru_conv_w, lru_cb, w_gates,
                      lru_ba, lru_bx, lru_lam, l, tm=512)
        y_gdn = gdn(proj, proj_small, qkv_col, z_col, ba_col, gdn_conv_w, gdn_alog, gdn_dtb, gdn_nw, l,
                    tm=512, exact_inverse=False)

        mixed = merge(y_mla, y_lru, y_gdn, proj, b_gate, w_branch_b, l, tm=512, tn=512)
        xs = residual_matmul(xs, mixed, w_out_b, l, tm=1024, tn=1024)
        xs = ffn(xs, gain_ffn, ffn_gate_b, ffn_up_b, ffn_down_b, gain_final, l, tm=512, th=512,
                 final_norm=(l == depth - 1))
    return xs.reshape(b, s, d)
```

```python
import functools
import math

import jax
import jax.numpy as jnp
from jax import lax
from jax.experimental import pallas as pl
from jax.experimental.pallas import tpu as pltpu

F32 = jnp.float32
BF16 = jnp.bfloat16
HIGHEST = lax.Precision.HIGHEST

EPS = 1e-6
N_BRANCH = 3
MLA_HEADS = 8
QK_NOPE = 128
QK_ROPE = 64
ROPE_HALF = QK_ROPE // 2
QK_DIM = QK_NOPE + QK_ROPE
V_HEAD = 128
Q_LORA = 512
KV_LORA = 512
ROPE_THETA = 10000.0
LRU_WIDTH = 1024
LRU_BLOCKS = 8
LRU_BLOCK_W = LRU_WIDTH // LRU_BLOCKS
LRU_C = 8.0
CONV_W = 4
GDN_HEADS = 8
GDN_DK = 128
GDN_DV = 128
GDN_CHUNK = 64
GDN_INV_BLOCK = 16

V7X_VMEM_BYTES = 64 * 1024 * 1024
VMEM_LIMIT_BYTES = V7X_VMEM_BYTES - 8 * 1024 * 1024
SUBLANES = 8
LANES = 128

GATE_COLS = N_BRANCH * 2048
GDN_QKV_COLS = 3 * GDN_HEADS * GDN_DK
SMALL_COLS = 1152
SMALL_ROPE_OFF = Q_LORA + KV_LORA
SMALL_B_OFF = SMALL_ROPE_OFF + QK_ROPE
SMALL_A_OFF = SMALL_B_OFF + GDN_HEADS


def _params(*semantics):
    return pltpu.CompilerParams(dimension_semantics=semantics, vmem_limit_bytes=VMEM_LIMIT_BYTES)


def _dot(a, b, precision=None):
    return jnp.dot(a, b, preferred_element_type=F32, precision=precision)


def _rms(x, gain):
    return x * lax.rsqrt(jnp.mean(x * x, axis=-1, keepdims=True) + EPS) * gain


def _softplus(x):
    return jnp.maximum(x, 0.0) + jnp.log1p(jnp.exp(-jnp.abs(x)))


def _silu(x):
    return x * jax.nn.sigmoid(x)


def _gelu_tanh(x):
    return 0.5 * x * (1.0 + jnp.tanh(math.sqrt(2.0 / math.pi) * (x + 0.044715 * (x * x * x))))


def _norm_matmul_kernel(x_ref, g_ref, w_ref, o_ref, h_ref):
    @pl.when(pl.program_id(1) == 0)
    def _():
        h_ref[...] = _rms(x_ref[...], g_ref[...]).astype(BF16)

    o_ref[...] = _dot(h_ref[...], w_ref[...]).astype(o_ref.dtype)


def _per_layer(a, layer):
    return pl.BlockSpec((None,) + a.shape[1:], lambda *_: (layer,) + (0,) * (a.ndim - 1))


def norm_matmul(x, gain, w, layer, *, tm, tn):
    s, k = x.shape
    n = w.shape[2]
    return pl.pallas_call(
        _norm_matmul_kernel,
        grid=(s // tm, n // tn),
        in_specs=[
            pl.BlockSpec((tm, k), lambda i, j: (i, 0)),
            _per_layer(gain, layer),
            pl.BlockSpec((None, k, tn), lambda i, j: (layer, 0, j)),
        ],
        out_specs=[pl.BlockSpec((tm, tn), lambda i, j: (i, j)),
                   pl.BlockSpec((tm, k), lambda i, j: (i, 0))],
        out_shape=[jax.ShapeDtypeStruct((s, n), F32), jax.ShapeDtypeStruct((s, k), BF16)],
        compiler_params=_params("arbitrary", "arbitrary"),
        name="norm_matmul",
    )(x, gain, w)


def _matmul_kernel(a_ref, w_ref, o_ref):
    o_ref[...] = _dot(a_ref[...], w_ref[...])


def matmul(a, w, layer, *, tm, tn):
    s, k = a.shape
    n = w.shape[2]
    return pl.pallas_call(
        _matmul_kernel,
        grid=(s // tm, n // tn),
        in_specs=[
            pl.BlockSpec((tm, k), lambda i, j: (i, 0)),
            pl.BlockSpec((None, k, tn), lambda i, j: (layer, 0, j)),
        ],
        out_specs=pl.BlockSpec((tm, tn), lambda i, j: (i, j)),
        out_shape=jax.ShapeDtypeStruct((s, n), F32),
        compiler_params=_params("arbitrary", "arbitrary"),
        name="matmul",
    )(a, w)


def _rope_table_kernel(pos_ref, inv_ref, cos_ref, sin_ref):
    ang = pos_ref[...] * inv_ref[...]
    cos_ref[...] = jnp.cos(ang)
    sin_ref[...] = jnp.sin(ang)


def rope_tables(pos_b, inv_b, *, tm):
    s, n = pos_b.shape
    blk = pl.BlockSpec((tm, n), lambda i: (i, 0))
    return pl.pallas_call(
        _rope_table_kernel,
        grid=(s // tm,),
        in_specs=[blk, pl.BlockSpec((1, n), lambda i: (0, 0))],
        out_specs=[blk, blk],
        out_shape=[jax.ShapeDtypeStruct((s, n), F32)] * 2,
        compiler_params=_params("arbitrary"),
        name="rope_tables",
    )(pos_b, inv_b)


def _mla_prep_kernel(p_ref, qg_ref, kvg_ref, wqn_ref, wq1_ref, wq2_ref, wk_ref, wv_ref,
                     cos_ref, sin_ref, q_out, k_out, v_out):
    p = p_ref[...]
    hq = _rms(p[:, :Q_LORA], qg_ref[...]).astype(BF16)
    hkv = _rms(p[:, Q_LORA:SMALL_ROPE_OFF], kvg_ref[...]).astype(BF16)
    kr = p[:, SMALL_ROPE_OFF:SMALL_ROPE_OFF + QK_ROPE]
    cos = cos_ref[...]
    sin = sin_ref[...]
    qn = _dot(hq, wqn_ref[...])
    q1 = _dot(hq, wq1_ref[...])
    q2 = _dot(hq, wq2_ref[...])
    qr1 = q1 * cos - q2 * sin
    qr2 = q2 * cos + q1 * sin
    kn = _dot(hkv, wk_ref[...])
    v = _dot(hkv, wv_ref[...])
    c32 = cos[:, :ROPE_HALF]
    s32 = sin[:, :ROPE_HALF]
    k1 = kr[:, :ROPE_HALF]
    k2 = kr[:, ROPE_HALF:]
    kr1 = k1 * c32 - k2 * s32
    kr2 = k2 * c32 + k1 * s32
    for h in range(MLA_HEADS):
        nope = slice(h * QK_NOPE, (h + 1) * QK_NOPE)
        rope = slice(h * ROPE_HALF, (h + 1) * ROPE_HALF)
        q_out[h] = jnp.concatenate([qn[:, nope], qr1[:, rope], qr2[:, rope]], axis=-1).astype(BF16)
        k_out[h] = jnp.concatenate([kn[:, nope], kr1, kr2], axis=-1).astype(BF16)
        v_out[h] = v[:, h * V_HEAD:(h + 1) * V_HEAD].astype(BF16)


def mla_prep(proj_small, q_gain, kv_gain, wqn, wq1, wq2, wk, wv, cos, sin, layer, *, tm):
    s = proj_small.shape[0]
    full = lambda a: _per_layer(a, layer)
    rows = lambda n: pl.BlockSpec((tm, n), lambda i: (i, 0))
    heads = lambda d: pl.BlockSpec((MLA_HEADS, tm, d), lambda i: (0, i, 0))
    return pl.pallas_call(
        _mla_prep_kernel,
        grid=(s // tm,),
        in_specs=[rows(SMALL_COLS), full(q_gain), full(kv_gain), full(wqn), full(wq1), full(wq2),
                  full(wk), full(wv), rows(cos.shape[1]), rows(sin.shape[1])],
        out_specs=[heads(QK_DIM), heads(QK_DIM), heads(V_HEAD)],
        out_shape=[jax.ShapeDtypeStruct((MLA_HEADS, s, QK_DIM), BF16),
                   jax.ShapeDtypeStruct((MLA_HEADS, s, QK_DIM), BF16),
                   jax.ShapeDtypeStruct((MLA_HEADS, s, V_HEAD), BF16)],
        compiler_params=_params("arbitrary"),
        name="mla_prep",
    )(proj_small, q_gain, kv_gain, wqn, wq1, wq2, wk, wv, cos, sin)


def _attention_kernel(q_ref, k_ref, v_ref, o_ref, *, tr, tk, n_sub, exp2_scale):
    i = pl.program_id(1)
    tq = tr * n_sub
    qs = [q_ref[0, a * tr:(a + 1) * tr, :] for a in range(n_sub)]

    def kv_block(j):
        start = pl.multiple_of(j * tk, tk)
        return k_ref[0, pl.ds(start, tk), :], v_ref[0, pl.ds(start, tk), :]

    def update(q, kv, carry, mask_offset):
        m, l, acc = carry
        kj, vj = kv
        s = lax.dot_general(q, kj, (((1,), (1,)), ((), ())), preferred_element_type=F32)
        if mask_offset is not None:
            row = lax.broadcasted_iota(jnp.int32, (tr, tk), 0)
            col = lax.broadcasted_iota(jnp.int32, (tr, tk), 1)
            s = jnp.where(col + mask_offset <= row, s, -jnp.inf)
        m_new = jnp.maximum(m, jnp.max(s, axis=-1, keepdims=True))
        p = jnp.exp2((s - m_new) * exp2_scale)
        alpha = jnp.exp2((m - m_new) * exp2_scale)
        l = alpha * l + jnp.sum(p, axis=-1, keepdims=True)
        acc = alpha * acc + _dot(p.astype(BF16), vj)
        return m_new, l, acc

    def full_blocks(j, carries):
        kv = kv_block(j)
        return tuple(update(qs[a], kv, carries[a], None) for a in range(n_sub))

    init = (jnp.full((tr, 1), -jnp.inf, F32), jnp.zeros((tr, 1), F32), jnp.zeros((tr, V_HEAD), F32))
    n_past = i * (tq // tk)
    carries = list(lax.fori_loop(0, n_past, full_blocks, (init,) * n_sub))
    for b in range(tq // tk):
        kv = kv_block(n_past + b)
        c0 = b * tk
        for a in range(n_sub):
            r0 = a * tr
            if c0 > r0 + tr - 1:
                continue
            whole = c0 + tk - 1 <= r0
            carries[a] = update(qs[a], kv, carries[a], None if whole else c0 - r0)
    for a in range(n_sub):
        _, l, acc = carries[a]
        o_ref[a * tr:(a + 1) * tr, :] = (acc / l).astype(o_ref.dtype)


def attention(q, k, v, *, tr, tk, n_sub):
    h, s, _ = q.shape
    tq = tr * n_sub
    assert tq % tk == 0
    exp2_scale = QK_DIM ** -0.5 * math.log2(math.e)
    return pl.pallas_call(
        functools.partial(_attention_kernel, tr=tr, tk=tk, n_sub=n_sub, exp2_scale=exp2_scale),
        grid=(h, s // tq),
        in_specs=[
            pl.BlockSpec((1, tq, QK_DIM), lambda hh, i: (hh, i, 0)),
            pl.BlockSpec((1, s, QK_DIM), lambda hh, i: (hh, 0, 0)),
            pl.BlockSpec((1, s, V_HEAD), lambda hh, i: (hh, 0, 0)),
        ],
        out_specs=pl.BlockSpec((tq, V_HEAD), lambda hh, i: (i, hh)),
        out_shape=jax.ShapeDtypeStruct((s, h * V_HEAD), BF16),
        compiler_params=_params("arbitrary", "arbitrary"),
        name="attention",
    )(q, k, v)


def _causal_conv_tile(x_refs, cw_ref, xbuf, tm):
    @pl.when(pl.program_id(0) == 0)
    def _():
        xbuf[0:SUBLANES, :] = jnp.zeros((SUBLANES, xbuf.shape[1]), F32)

    off = 0
    for x_ref in x_refs:
        xbuf[SUBLANES:SUBLANES + tm, off:off + x_ref.shape[1]] = x_ref[...]
        off += x_ref.shape[1]
    cw = cw_ref[...]
    acc = cw[0:1, :] * xbuf[pl.ds(SUBLANES - CONV_W + 1, tm), :]
    for k in range(1, CONV_W):
        acc = acc + cw[k:k + 1, :] * xbuf[pl.ds(SUBLANES - CONV_W + 1 + k, tm), :]
    xbuf[0:SUBLANES, :] = xbuf[tm:tm + SUBLANES, :]
    return acc


def _rglru_kernel(x_ref, y_ref, cw_ref, cb_ref, wg_ref, ba_ref, bx_ref, lam_ref, o_ref,
                  xbuf, a_s, b_s, h_carry, *, tm):
    @pl.when(pl.program_id(0) == 0)
    def _():
        h_carry[...] = jnp.zeros(h_carry.shape, F32)

    xc = _causal_conv_tile([x_ref], cw_ref, xbuf, tm) + cb_ref[...]
    xcb = xc.astype(BF16)
    pair_w = 2 * LRU_BLOCK_W
    r_parts, i_parts = [], []
    for p in range(LRU_BLOCKS // 2):
        g = _dot(xcb[:, p * pair_w:(p + 1) * pair_w], wg_ref[p])
        r_parts.append(g[:, :pair_w])
        i_parts.append(g[:, pair_w:])
    r = jax.nn.sigmoid(jnp.concatenate(r_parts, axis=-1) + ba_ref[...])
    gi = jax.nn.sigmoid(jnp.concatenate(i_parts, axis=-1) + bx_ref[...])
    log_a = -LRU_C * r * _softplus(-lam_ref[...])
    t = jnp.tanh(log_a)
    mult = jnp.sqrt(-2.0 * t / (1.0 - t))
    a_s[...] = jnp.exp(log_a)
    b_s[...] = mult * (gi * xc)

    row = lax.broadcasted_iota(jnp.int32, (SUBLANES, LRU_WIDTH), 0)

    def group(gidx, h_prev):
        r0 = pl.multiple_of(gidx * SUBLANES, SUBLANES)
        a = a_s[pl.ds(r0, SUBLANES), :]
        b = b_s[pl.ds(r0, SUBLANES), :]
        shift = 1
        while shift < SUBLANES:
            keep = row >= shift
            a_sh = jnp.where(keep, pltpu.roll(a, shift, 0), 1.0)
            b_sh = jnp.where(keep, pltpu.roll(b, shift, 0), 0.0)
            b = a * b_sh + b
            a = a * a_sh
            shift *= 2
        hs = a * h_prev + b
        b_s[pl.ds(r0, SUBLANES), :] = hs
        return hs[SUBLANES - 1:SUBLANES, :]

    h_carry[...] = lax.fori_loop(0, tm // SUBLANES, group, h_carry[...])
    o_ref[...] = (b_s[...] * _gelu_tanh(y_ref[...])).astype(o_ref.dtype)


def rglru(proj, x_col, y_col, conv_w, conv_b, w_gates, b_a, b_x, lam, layer, *, tm):
    s = proj.shape[0]
    full = lambda a: _per_layer(a, layer)
    return pl.pallas_call(
        functools.partial(_rglru_kernel, tm=tm),
        grid=(s // tm,),
        in_specs=[
            pl.BlockSpec((tm, LRU_WIDTH), lambda i: (i, x_col)),
            pl.BlockSpec((tm, LRU_WIDTH), lambda i: (i, y_col)),
            full(conv_w), full(conv_b), full(w_gates), full(b_a), full(b_x), full(lam),
        ],
        out_specs=pl.BlockSpec((tm, LRU_WIDTH), lambda i: (i, 0)),
        out_shape=jax.ShapeDtypeStruct((s, LRU_WIDTH), BF16),
        scratch_shapes=[
            pltpu.VMEM((tm + SUBLANES, LRU_WIDTH), F32),
            pltpu.VMEM((tm, LRU_WIDTH), F32),
            pltpu.VMEM((tm, LRU_WIDTH), F32),
            pltpu.VMEM((1, LRU_WIDTH), F32),
        ],
        compiler_params=_params("arbitrary"),
        name="rglru",
    )(proj, proj, conv_w, conv_b, w_gates, b_a, b_x, lam)


def _unit_lower_solve(l_mats, rhs, block_diag, cast, mm):
    heads = range(len(l_mats))
    d = [jnp.where(block_diag, l_mats[h], 0.0) for h in heads]
    n = [l_mats[h] - d[h] for h in heads]
    dc = [cast(d[h]) for h in heads]
    d2 = [mm(dc[h], dc[h]) for h in heads]
    d2c = [cast(d2[h]) for h in heads]
    d4 = [mm(d2c[h], d2c[h]) for h in heads]
    d4c = [cast(d4[h]) for h in heads]
    d8 = [mm(d4c[h], d4c[h]) for h in heads]
    d8c = [cast(d8[h]) for h in heads]
    x = [-d[h] for h in heads]
    for dp, dpc in ((d2, d2c), (d4, d4c), (d8, d8c)):
        x = [x[h] + dp[h] + mm(cast(x[h]), dpc[h]) for h in heads]
    xc = [cast(x[h]) for h in heads]
    m = [n[h] + mm(xc[h], cast(n[h])) for h in heads]
    mc = [cast(m[h]) for h in heads]
    m2 = [mm(mc[h], mc[h]) for h in heads]
    y = [m2[h] - m[h] - mm(mc[h], cast(m2[h])) for h in heads]
    r1 = [rhs[h] + mm(xc[h], cast(rhs[h])) for h in heads]
    return [r1[h] + mm(cast(y[h]), cast(r1[h])) for h in heads]


def _gdn_kernel(qin_ref, kin_ref, vin_ref, z_ref, ba_ref, cw_ref, alog_ref, dtb_ref, nw_ref, o_ref,
                xbuf, q_s, k_s, v_s, g_s, beta_s, u_s, wq_s, kt_s, qk_s, el_s, state,
                *, tm, chunks_per_step, exact_inverse):
    @pl.when(pl.program_id(0) == 0)
    def _():
        state[...] = jnp.zeros(state.shape, F32)

    act = _silu(_causal_conv_tile([qin_ref, kin_ref, vin_ref], cw_ref, xbuf, tm))
    kw = GDN_HEADS * GDN_DK
    for h in range(GDN_HEADS):
        sl = slice(h * GDN_DK, (h + 1) * GDN_DK)
        qh = act[:, h * GDN_DK:(h + 1) * GDN_DK]
        kh = act[:, kw + h * GDN_DK:kw + (h + 1) * GDN_DK]
        q_s[:, sl] = qh * lax.rsqrt(jnp.sum(qh * qh, axis=-1, keepdims=True) + EPS) * (GDN_DK ** -0.5)
        k_s[:, sl] = kh * lax.rsqrt(jnp.sum(kh * kh, axis=-1, keepdims=True) + EPS)
    v_s[...] = act[:, 2 * kw:]
    ba = ba_ref[...]
    beta_s[...] = jax.nn.sigmoid(ba[:, SMALL_B_OFF % LANES:SMALL_B_OFF % LANES + GDN_HEADS])
    a_logit = ba[:, SMALL_A_OFF % LANES:SMALL_A_OFF % LANES + GDN_HEADS]
    g_s[...] = -jnp.exp(alog_ref[...]) * _softplus(a_logit + dtb_ref[...])

    c = GDN_CHUNK
    row = lax.broadcasted_iota(jnp.int32, (c, c), 0)
    col = lax.broadcasted_iota(jnp.int32, (c, c), 1)
    incl = row >= col
    strict = row > col
    tril = incl.astype(F32)
    stril = strict.astype(F32)
    block_diag = jnp.logical_and(strict, (row // GDN_INV_BLOCK) == (col // GDN_INV_BLOCK))
    norm_w = nw_ref[...]
    nt_dims = (((1,), (1,)), ((), ()))
    tn_dims = (((0,), (0,)), ((), ()))
    heads = range(GDN_HEADS)
    hs = lambda h: slice(h * GDN_DK, (h + 1) * GDN_DK)
    cs = lambda h: slice(h * c, (h + 1) * c)
    bf = lambda t: t.astype(BF16)
    if exact_inverse:
        cast = lambda t: t
        mm = lambda a, b: _dot(a, b, HIGHEST)
    else:
        cast = bf
        mm = _dot

    def chunk_operands(ci):
        r0 = pl.multiple_of(ci * c, c)
        rows = pl.ds(r0, c)
        g_c = g_s[rows, :]
        beta_c = beta_s[rows, :]
        gc = _dot(tril, g_c, HIGHEST)
        gb = jnp.concatenate([jnp.broadcast_to(g_c[:, h:h + 1], (c, c)) * stril for h in heads], axis=1)
        diff = _dot(tril, gb, HIGHEST)
        e_gc = jnp.exp(gc)
        g_last = gc[c - 1:c, :]
        e_tail = jnp.exp(g_last - gc)
        el_s[pl.ds(ci, 1), :] = jnp.exp(g_last)
        q = [q_s[rows, hs(h)] for h in heads]
        k = [k_s[rows, hs(h)] for h in heads]
        bh = [jnp.broadcast_to(beta_c[:, h:h + 1], (c, GDN_DK)) for h in heads]
        eg = [jnp.broadcast_to(e_gc[:, h:h + 1], (c, GDN_DK)) for h in heads]
        kb = [k[h] * bh[h] for h in heads]
        decay = [jnp.where(incl, jnp.exp(diff[:, cs(h)]), 0.0) for h in heads]
        kq = [lax.dot_general(bf(jnp.concatenate([kb[h], q[h]], axis=0)), bf(k[h]), nt_dims,
                              preferred_element_type=F32) for h in heads]
        l_mats = [jnp.where(strict, kq[h][:c] * decay[h], 0.0) for h in heads]
        for h in heads:
            qk_s[rows, cs(h)] = bf(kq[h][c:] * decay[h])
            kt_s[rows, hs(h)] = bf(k[h] * jnp.broadcast_to(e_tail[:, h:h + 1], (c, GDN_DK)))
        rhs = [jnp.concatenate([v_s[rows, hs(h)] * bh[h], kb[h] * eg[h]], axis=1) for h in heads]
        q_dec = [q[h] * eg[h] for h in heads]
        return r0, l_mats, rhs, q_dec

    def prepare(step, carry):
        parts = [chunk_operands(step * chunks_per_step + t) for t in range(chunks_per_step)]
        l_mats = [l for part in parts for l in part[1]]
        rhs = [r for part in parts for r in part[2]]
        uw = _unit_lower_solve(l_mats, rhs, block_diag, cast, mm)
        for t, (r0, _, _, q_dec) in enumerate(parts):
            rows = pl.ds(r0, c)
            rows_w = pl.ds(pl.multiple_of(2 * r0, 2 * c), c)
            rows_q = pl.ds(pl.multiple_of(2 * r0 + c, c), c)
            for h in heads:
                uw_h = uw[t * GDN_HEADS + h]
                u_s[rows, hs(h)] = uw_h[:, :GDN_DV]
                wq_s[rows_w, hs(h)] = bf(uw_h[:, GDN_DV:])
                wq_s[rows_q, hs(h)] = bf(q_dec[h])
        return carry

    lax.fori_loop(0, tm // (c * chunks_per_step), prepare, 0)

    def advance(ci, carry):
        r0 = pl.multiple_of(ci * c, c)
        rows = pl.ds(r0, c)
        rows_wq = pl.ds(pl.multiple_of(2 * r0, 2 * c), 2 * c)
        e_last = el_s[pl.ds(ci, 1), :]
        s = [state[h] for h in heads]
        ws_qs = [_dot(wq_s[rows_wq, hs(h)], bf(s[h])) for h in heads]
        v_new = [bf(u_s[rows, hs(h)] - ws_qs[h][:c]) for h in heads]
        o = [ws_qs[h][c:] + _dot(qk_s[rows, cs(h)], v_new[h]) for h in heads]
        d_state = [lax.dot_general(kt_s[rows, hs(h)], v_new[h], tn_dims, preferred_element_type=F32)
                   for h in heads]
        for h in heads:
            state[h] = s[h] * e_last[:, h:h + 1] + d_state[h]
        for h in heads:
            o_ref[rows, hs(h)] = (_rms(o[h], norm_w) * _silu(z_ref[rows, hs(h)])).astype(o_ref.dtype)
        return carry

    lax.fori_loop(0, tm // c, advance, 0)


def gdn(proj, proj_small, q_col, ba_col, conv_w, a_log, dt_bias, norm_w, layer, *, tm, chunks_per_step,
        exact_inverse):
    s = proj.shape[0]
    width = GDN_HEADS * GDN_DV
    per_layer = lambda a: _per_layer(a, layer)
    col = lambda c: pl.BlockSpec((tm, width), lambda i: (i, c))
    return pl.pallas_call(
        functools.partial(_gdn_kernel, tm=tm, chunks_per_step=chunks_per_step,
                          exact_inverse=exact_inverse),
        grid=(s // tm,),
        in_specs=[
            col(q_col), col(q_col + 1), col(q_col + 2), col(q_col + 3),
            pl.BlockSpec((tm, LANES), lambda i: (i, ba_col)),
            per_layer(conv_w), per_layer(a_log), per_layer(dt_bias), per_layer(norm_w),
        ],
        out_specs=pl.BlockSpec((tm, width), lambda i: (i, 0)),
        out_shape=jax.ShapeDtypeStruct((s, width), BF16),
        scratch_shapes=[
            pltpu.VMEM((tm + SUBLANES, GDN_QKV_COLS), F32),
            pltpu.VMEM((tm, width), F32),
            pltpu.VMEM((tm, width), F32),
            pltpu.VMEM((tm, width), F32),
            pltpu.VMEM((tm, GDN_HEADS), F32),
            pltpu.VMEM((tm, GDN_HEADS), F32),
            pltpu.VMEM((tm, width), F32),
            pltpu.VMEM((2 * tm, width), BF16),
            pltpu.VMEM((tm, width), BF16),
            pltpu.VMEM((tm, GDN_HEADS * GDN_CHUNK), BF16),
            pltpu.VMEM((tm // GDN_CHUNK, GDN_HEADS), F32),
            pltpu.VMEM((GDN_HEADS, GDN_DK, GDN_DV), F32),
        ],
        compiler_params=_params("arbitrary"),
        name="gdn",
    )(proj, proj, proj, proj, proj_small, conv_w, a_log, dt_bias, norm_w)


def _merge_kernel(h_ref, ym_ref, yl_ref, yg_ref, wg0_ref, wg1_ref, wg2_ref, bg_ref, wb_ref, o_ref):
    bg = bg_ref[...]
    h = h_ref[...]
    acc = None
    for n, (y_ref, wg_ref) in enumerate(((ym_ref, wg0_ref), (yl_ref, wg1_ref), (yg_ref, wg2_ref))):
        gate = jax.nn.sigmoid(_dot(h, wg_ref[...]) + bg[n:n + 1, :])
        term = gate * _dot(y_ref[...], wb_ref[n])
        acc = term if acc is None else acc + term
    o_ref[...] = acc.astype(o_ref.dtype)


def merge(h, y_mla, y_lru, y_gdn, w_gate, b_gate, w_branch, layer, *, tm, tn):
    s, bw = y_mla.shape
    k = h.shape[1]
    d = w_branch.shape[3]
    nj = d // tn
    y_spec = pl.BlockSpec((tm, bw), lambda i, j: (i, 0))
    gate_spec = lambda n: pl.BlockSpec((None, k, tn), lambda i, j: (layer, 0, n * nj + j))
    return pl.pallas_call(
        _merge_kernel,
        grid=(s // tm, nj),
        in_specs=[pl.BlockSpec((tm, k), lambda i, j: (i, 0)), y_spec, y_spec, y_spec,
                  gate_spec(0), gate_spec(1), gate_spec(2),
                  pl.BlockSpec((None, N_BRANCH, tn), lambda i, j: (layer, 0, j)),
                  pl.BlockSpec((None, N_BRANCH, bw, tn), lambda i, j: (layer, 0, 0, j))],
        out_specs=pl.BlockSpec((tm, tn), lambda i, j: (i, j)),
        out_shape=jax.ShapeDtypeStruct((s, d), BF16),
        compiler_params=_params("arbitrary", "arbitrary"),
        name="merge",
    )(h, y_mla, y_lru, y_gdn, w_gate, w_gate, w_gate, b_gate, w_branch)


def _residual_matmul_kernel(x_ref, a_ref, w_ref, o_ref):
    o_ref[...] = x_ref[...] + _dot(a_ref[...], w_ref[...])


def residual_matmul(x, a, w, layer, *, tm, tn):
    s, k = a.shape
    n = w.shape[2]
    return pl.pallas_call(
        _residual_matmul_kernel,
        grid=(s // tm, n // tn),
        in_specs=[
            pl.BlockSpec((tm, tn), lambda i, j: (i, j)),
            pl.BlockSpec((tm, k), lambda i, j: (i, 0)),
            pl.BlockSpec((None, k, tn), lambda i, j: (layer, 0, j)),
        ],
        out_specs=pl.BlockSpec((tm, tn), lambda i, j: (i, j)),
        out_shape=jax.ShapeDtypeStruct((s, n), F32),
        compiler_params=_params("arbitrary", "arbitrary"),
        name="residual_matmul",
    )(x, a, w)


def _ffn_kernel(x_ref, g_ref, wg_ref, wu_ref, wd_ref, fg_ref, o_ref, h_ref, a_ref,
                *, n_hidden, th, n_out, tn, final_norm):
    j = pl.program_id(1)

    @pl.when(j == 0)
    def _():
        h_ref[...] = _rms(x_ref[...], g_ref[...]).astype(BF16)

    @pl.when(j < n_hidden)
    def _():
        h = h_ref[...]
        a = (_silu(_dot(h, wg_ref[...])) * _dot(h, wu_ref[...])).astype(BF16)
        for t in range(n_hidden):
            @pl.when(j == t)
            def _():
                a_ref[:, t * th:(t + 1) * th] = a

    for t in range(n_out):
        @pl.when(j == n_hidden + t)
        def _():
            cols = slice(t * tn, (t + 1) * tn)
            o_ref[:, cols] = x_ref[:, cols] + _dot(a_ref[...], wd_ref[...])

    if final_norm:
        @pl.when(j == n_hidden + n_out - 1)
        def _():
            o_ref[...] = _rms(o_ref[...], fg_ref[...])


def ffn(x, gain, w_gate, w_up, w_down, final_gain, layer, *, tm, th, tn, final_norm):
    s, d = x.shape
    hidden = w_gate.shape[2]
    n_hidden, n_out = hidden // th, d // tn
    up_spec = pl.BlockSpec((None, d, th), lambda i, j: (layer, 0, jnp.minimum(j, n_hidden - 1)))
    return pl.pallas_call(
        functools.partial(_ffn_kernel, n_hidden=n_hidden, th=th, n_out=n_out, tn=tn,
                          final_norm=final_norm),
        grid=(s // tm, n_hidden + n_out),
        in_specs=[
            pl.BlockSpec((tm, d), lambda i, j: (i, 0)),
            _per_layer(gain, layer),
            up_spec,
            up_spec,
            pl.BlockSpec((None, hidden, tn), lambda i, j: (layer, 0, jnp.maximum(j - n_hidden, 0))),
            pl.BlockSpec((1, d), lambda i, j: (0, 0)),
        ],
        out_specs=pl.BlockSpec((tm, d), lambda i, j: (i, 0)),
        out_shape=jax.ShapeDtypeStruct((s, d), F32),
        scratch_shapes=[pltpu.VMEM((tm, d), BF16), pltpu.VMEM((tm, hidden), BF16)],
        compiler_params=_params("arbitrary", "arbitrary"),
        name="ffn",
    )(x, gain, w_gate, w_up, w_down, final_gain)


def _split_w_in(w):
    sizes = [Q_LORA, KV_LORA, QK_ROPE, LRU_WIDTH, LRU_WIDTH,
             GDN_HEADS * GDN_DK, GDN_HEADS * GDN_DK, GDN_HEADS * GDN_DV, GDN_HEADS * GDN_DV,
             GDN_HEADS, GDN_HEADS, GATE_COLS]
    parts, off = [], 0
    for n in sizes:
        parts.append(w[:, :, off:off + n])
        off += n
    return parts


def _regroup_weights(w_in, w_uq, w_ukv, lru_w_a, lru_w_x):
    (c_q, c_kv, k_rope, lru_x, lru_y, g_q, g_k, g_v, g_z, g_b, g_a, gates) = _split_w_in(w_in)
    depth, d, _ = w_in.shape
    w_main = jnp.concatenate([lru_x, lru_y, g_q, g_k, g_v, g_z], axis=2).astype(BF16)
    w_gate = gates.astype(BF16)
    pad = jnp.zeros((depth, d, SMALL_COLS - SMALL_A_OFF - GDN_HEADS), w_in.dtype)
    w_small = jnp.concatenate([c_q, c_kv, k_rope, g_b, g_a, pad], axis=2).astype(BF16)

    uq = w_uq.reshape(depth, Q_LORA, MLA_HEADS, QK_DIM)
    flat = lambda t: t.reshape(depth, t.shape[1], -1).astype(BF16)
    wqn = flat(uq[..., :QK_NOPE])
    wq1 = flat(uq[..., QK_NOPE:QK_NOPE + ROPE_HALF])
    wq2 = flat(uq[..., QK_NOPE + ROPE_HALF:])
    ukv = w_ukv.reshape(depth, KV_LORA, MLA_HEADS, QK_NOPE + V_HEAD)
    wk = flat(ukv[..., :QK_NOPE])
    wv = flat(ukv[..., QK_NOPE:])

    def pair_tiles(w):
        pairs = w.reshape(depth, LRU_BLOCKS // 2, 2, LRU_BLOCK_W, LRU_BLOCK_W)
        z = jnp.zeros_like(pairs[:, :, 0])
        top = jnp.concatenate([pairs[:, :, 0], z], axis=-1)
        bottom = jnp.concatenate([z, pairs[:, :, 1]], axis=-1)
        return jnp.concatenate([top, bottom], axis=-2)

    w_gates = jnp.concatenate([pair_tiles(lru_w_a), pair_tiles(lru_w_x)], axis=-1).astype(BF16)
    return w_main, w_gate, w_small, wqn, wq1, wq2, wk, wv, w_gates


def _rows(v):
    return v.reshape(v.shape[0], 1, -1)


def kernel(x, positions, norm_mix, w_in, mla_q_norm, mla_w_uq, mla_kv_norm, mla_w_ukv, lru_conv_w, lru_conv_b, lru_w_a, lru_b_a, lru_w_x, lru_b_x, lru_lambda, gdn_conv_w, gdn_a_log, gdn_dt_bias, gdn_norm, w_branch, b_gate, w_out, norm_ffn, ffn_w_gate, ffn_w_up, ffn_w_down, norm_final):
    b, s, d = x.shape
    assert b == 1
    depth = w_in.shape[0]
    xs = x.reshape(s, d)

    half = ROPE_HALF
    inv_freq = ROPE_THETA ** (-jnp.arange(half, dtype=F32) / half)
    inv_b = jnp.tile(inv_freq, MLA_HEADS).reshape(1, MLA_HEADS * half)
    pos_b = jnp.broadcast_to(positions.astype(F32).reshape(s, 1), (s, MLA_HEADS * half))
    cos, sin = rope_tables(pos_b, inv_b, tm=1024)

    lru_x_col, gdn_q_col = 0, 2
    ba_col = SMALL_ROPE_OFF // LANES

    w_main, w_gate, w_small, wqn, wq1, wq2, wk, wv, w_gates = _regroup_weights(
        w_in, mla_w_uq, mla_w_ukv, lru_w_a, lru_w_x)
    w_branch_b = w_branch.astype(BF16)
    w_out_b = w_out.astype(BF16)
    ffn_gate_b = ffn_w_gate.astype(BF16)
    ffn_up_b = ffn_w_up.astype(BF16)
    ffn_down_b = ffn_w_down.astype(BF16)
    gain_mix, gain_ffn = _rows(norm_mix), _rows(norm_ffn)
    gain_q, gain_kv = _rows(mla_q_norm), _rows(mla_kv_norm)
    lru_cb, lru_ba, lru_bx, lru_lam = _rows(lru_conv_b), _rows(lru_b_a), _rows(lru_b_x), _rows(lru_lambda)
    gdn_alog, gdn_dtb, gdn_nw = _rows(gdn_a_log), _rows(gdn_dt_bias), _rows(gdn_norm)
    gain_final = norm_final.reshape(1, d)

    for l in range(depth):
        proj_small, h = norm_matmul(xs, gain_mix, w_small, l, tm=1024, tn=SMALL_COLS)
        proj = matmul(h, w_main, l, tm=1024, tn=1024)

        q, k, v = mla_prep(proj_small, gain_q, gain_kv, wqn, wq1, wq2, wk, wv, cos, sin, l, tm=512)
        y_mla = attention(q, k, v, tr=1024, tk=1024, n_sub=2)
        y_lru = rglru(proj, lru_x_col, lru_x_col + 1, lru_conv_w, lru_cb, w_gates,
                      lru_ba, lru_bx, lru_lam, l, tm=512)
        y_gdn = gdn(proj, proj_small, gdn_q_col, ba_col, gdn_conv_w, gdn_alog, gdn_dtb, gdn_nw, l,
                    tm=512, chunks_per_step=2, exact_inverse=False)

        mixed = merge(h, y_mla, y_lru, y_gdn, w_gate, b_gate, w_branch_b, l, tm=1024, tn=512)
        xs = residual_matmul(xs, mixed, w_out_b, l, tm=1024, tn=1024)
        xs = ffn(xs, gain_ffn, ffn_gate_b, ffn_up_b, ffn_down_b, gain_final, l, tm=512, th=512, tn=512,
                 final_norm=(l == depth - 1))
    return xs.reshape(b, s, d)
```

```python
import functools
import math

import jax
import jax.numpy as jnp
from jax import lax
from jax.experimental import pallas as pl
from jax.experimental.pallas import tpu as pltpu

F32 = jnp.float32
BF16 = jnp.bfloat16
HIGHEST = lax.Precision.HIGHEST

EPS = 1e-6
N_BRANCH = 3
MLA_HEADS = 8
QK_NOPE = 128
QK_ROPE = 64
ROPE_HALF = QK_ROPE // 2
QK_DIM = QK_NOPE + QK_ROPE
V_HEAD = 128
Q_LORA = 512
KV_LORA = 512
ROPE_THETA = 10000.0
LRU_WIDTH = 1024
LRU_BLOCKS = 8
LRU_BLOCK_W = LRU_WIDTH // LRU_BLOCKS
LRU_C = 8.0
CONV_W = 4
GDN_HEADS = 8
GDN_DK = 128
GDN_DV = 128
GDN_CHUNK = 64
GDN_INV_BLOCK = 16

V7X_VMEM_BYTES = 64 * 1024 * 1024
VMEM_LIMIT_BYTES = V7X_VMEM_BYTES - 8 * 1024 * 1024
SUBLANES = 8
LANES = 128

MAIN_TM, MAIN_TN = 2048, 1024
MERGE_TM, MERGE_TN = 1024, 512
OUT_TM, OUT_TN = 1024, 1024
FFN_TM, FFN_TH = 512, 512

GATE_COLS = N_BRANCH * 2048
GDN_QKV_COLS = 3 * GDN_HEADS * GDN_DK
SMALL_COLS = 1152
SMALL_ROPE_OFF = Q_LORA + KV_LORA
SMALL_B_OFF = SMALL_ROPE_OFF + QK_ROPE
SMALL_A_OFF = SMALL_B_OFF + GDN_HEADS


def _params(*semantics):
    return pltpu.CompilerParams(dimension_semantics=semantics, vmem_limit_bytes=VMEM_LIMIT_BYTES)


def _dot(a, b, precision=None):
    return jnp.dot(a, b, preferred_element_type=F32, precision=precision)


def _rms(x, gain):
    return x * lax.rsqrt(jnp.mean(x * x, axis=-1, keepdims=True) + EPS) * gain


def _softplus(x):
    return jnp.maximum(x, 0.0) + jnp.log1p(jnp.exp(-jnp.abs(x)))


def _silu(x):
    return x * jax.nn.sigmoid(x)


def _gelu_tanh(x):
    return 0.5 * x * (1.0 + jnp.tanh(math.sqrt(2.0 / math.pi) * (x + 0.044715 * (x * x * x))))


def _norm_matmul_kernel(x_ref, g_ref, w_ref, o_ref, h_ref):
    @pl.when(pl.program_id(1) == 0)
    def _():
        h_ref[...] = _rms(x_ref[...], g_ref[...]).astype(BF16)

    o_ref[...] = _dot(h_ref[...], w_ref[...]).astype(o_ref.dtype)


def _per_layer(a, layer):
    return pl.BlockSpec((None,) + a.shape[1:], lambda *_: (layer,) + (0,) * (a.ndim - 1))


def norm_matmul(x, gain, w, layer, *, tm, tn):
    s, k = x.shape
    n = w.shape[2]
    return pl.pallas_call(
        _norm_matmul_kernel,
        grid=(s // tm, n // tn),
        in_specs=[
            pl.BlockSpec((tm, k), lambda i, j: (i, 0)),
            _per_layer(gain, layer),
            pl.BlockSpec((None, k, tn), lambda i, j: (layer, 0, j)),
        ],
        out_specs=[pl.BlockSpec((tm, tn), lambda i, j: (i, j)),
                   pl.BlockSpec((tm, k), lambda i, j: (i, 0))],
        out_shape=[jax.ShapeDtypeStruct((s, n), F32), jax.ShapeDtypeStruct((s, k), BF16)],
        compiler_params=_params("arbitrary", "arbitrary"),
        name="norm_matmul",
    )(x, gain, w)


def _matmul_kernel(a_ref, w_ref, o_ref):
    o_ref[...] = _dot(a_ref[...], w_ref[...])


def matmul(a, w, layer, *, tm, tn):
    s, k = a.shape
    n = w.shape[2]
    return pl.pallas_call(
        _matmul_kernel,
        grid=(s // tm, n // tn),
        in_specs=[
            pl.BlockSpec((tm, k), lambda i, j: (i, 0)),
            pl.BlockSpec((None, k, tn), lambda i, j: (layer, 0, j)),
        ],
        out_specs=pl.BlockSpec((tm, tn), lambda i, j: (i, j)),
        out_shape=jax.ShapeDtypeStruct((s, n), F32),
        compiler_params=_params("arbitrary", "arbitrary"),
        name="matmul",
    )(a, w)


def _rope_table_kernel(pos_ref, inv_ref, cos_ref, sin_ref):
    ang = pos_ref[...] * inv_ref[...]
    cos_ref[...] = jnp.cos(ang)
    sin_ref[...] = jnp.sin(ang)


def rope_tables(pos_b, inv_b, *, tm):
    s, n = pos_b.shape
    blk = pl.BlockSpec((tm, n), lambda i: (i, 0))
    return pl.pallas_call(
        _rope_table_kernel,
        grid=(s // tm,),
        in_specs=[blk, pl.BlockSpec((1, n), lambda i: (0, 0))],
        out_specs=[blk, blk],
        out_shape=[jax.ShapeDtypeStruct((s, n), F32)] * 2,
        compiler_params=_params("arbitrary"),
        name="rope_tables",
    )(pos_b, inv_b)


def _mla_prep_kernel(p_ref, qg_ref, kvg_ref, wqn_ref, wq1_ref, wq2_ref, wk_ref, wv_ref,
                     cos_ref, sin_ref, q_out, k_out, v_out):
    p = p_ref[...]
    hq = _rms(p[:, :Q_LORA], qg_ref[...]).astype(BF16)
    hkv = _rms(p[:, Q_LORA:SMALL_ROPE_OFF], kvg_ref[...]).astype(BF16)
    kr = p[:, SMALL_ROPE_OFF:SMALL_ROPE_OFF + QK_ROPE]
    cos = cos_ref[...]
    sin = sin_ref[...]
    qn = _dot(hq, wqn_ref[...])
    q1 = _dot(hq, wq1_ref[...])
    q2 = _dot(hq, wq2_ref[...])
    qr1 = q1 * cos - q2 * sin
    qr2 = q2 * cos + q1 * sin
    kn = _dot(hkv, wk_ref[...])
    v = _dot(hkv, wv_ref[...])
    c32 = cos[:, :ROPE_HALF]
    s32 = sin[:, :ROPE_HALF]
    k1 = kr[:, :ROPE_HALF]
    k2 = kr[:, ROPE_HALF:]
    kr1 = k1 * c32 - k2 * s32
    kr2 = k2 * c32 + k1 * s32
    for h in range(MLA_HEADS):
        nope = slice(h * QK_NOPE, (h + 1) * QK_NOPE)
        rope = slice(h * ROPE_HALF, (h + 1) * ROPE_HALF)
        q_out[h] = jnp.concatenate([qn[:, nope], qr1[:, rope], qr2[:, rope]], axis=-1).astype(BF16)
        k_out[h] = jnp.concatenate([kn[:, nope], kr1, kr2], axis=-1).astype(BF16)
        v_out[h] = v[:, h * V_HEAD:(h + 1) * V_HEAD].astype(BF16)


def mla_prep(proj_small, q_gain, kv_gain, wqn, wq1, wq2, wk, wv, cos, sin, layer, *, tm):
    s = proj_small.shape[0]
    full = lambda a: _per_layer(a, layer)
    rows = lambda n: pl.BlockSpec((tm, n), lambda i: (i, 0))
    heads = lambda d: pl.BlockSpec((MLA_HEADS, tm, d), lambda i: (0, i, 0))
    return pl.pallas_call(
        _mla_prep_kernel,
        grid=(s // tm,),
        in_specs=[rows(SMALL_COLS), full(q_gain), full(kv_gain), full(wqn), full(wq1), full(wq2),
                  full(wk), full(wv), rows(cos.shape[1]), rows(sin.shape[1])],
        out_specs=[heads(QK_DIM), heads(QK_DIM), heads(V_HEAD)],
        out_shape=[jax.ShapeDtypeStruct((MLA_HEADS, s, QK_DIM), BF16),
                   jax.ShapeDtypeStruct((MLA_HEADS, s, QK_DIM), BF16),
                   jax.ShapeDtypeStruct((MLA_HEADS, s, V_HEAD), BF16)],
        compiler_params=_params("arbitrary"),
        name="mla_prep",
    )(proj_small, q_gain, kv_gain, wqn, wq1, wq2, wk, wv, cos, sin)


def _attention_kernel(q_ref, k_ref, v_ref, o_ref, *, tr, tk, n_sub, exp2_scale):
    i = pl.program_id(1)
    tq = tr * n_sub
    qs = [q_ref[0, a * tr:(a + 1) * tr, :] for a in range(n_sub)]

    def kv_block(j):
        start = pl.multiple_of(j * tk, tk)
        return k_ref[0, pl.ds(start, tk), :], v_ref[0, pl.ds(start, tk), :]

    def update(q, kv, carry, mask_offset):
        m, l, acc = carry
        kj, vj = kv
        s = lax.dot_general(q, kj, (((1,), (1,)), ((), ())), preferred_element_type=F32)
        if mask_offset is not None:
            row = lax.broadcasted_iota(jnp.int32, (tr, tk), 0)
            col = lax.broadcasted_iota(jnp.int32, (tr, tk), 1)
            s = jnp.where(col + mask_offset <= row, s, -jnp.inf)
        m_new = jnp.maximum(m, jnp.max(s, axis=-1, keepdims=True))
        p = jnp.exp2((s - m_new) * exp2_scale)
        alpha = jnp.exp2((m - m_new) * exp2_scale)
        l = alpha * l + jnp.sum(p, axis=-1, keepdims=True)
        acc = alpha * acc + _dot(p.astype(BF16), vj)
        return m_new, l, acc

    def full_blocks(j, carries):
        kv = kv_block(j)
        return tuple(update(qs[a], kv, carries[a], None) for a in range(n_sub))

    init = (jnp.full((tr, 1), -jnp.inf, F32), jnp.zeros((tr, 1), F32), jnp.zeros((tr, V_HEAD), F32))
    n_past = i * (tq // tk)
    carries = list(lax.fori_loop(0, n_past, full_blocks, (init,) * n_sub))
    for b in range(tq // tk):
        kv = kv_block(n_past + b)
        c0 = b * tk
        for a in range(n_sub):
            r0 = a * tr
            if c0 > r0 + tr - 1:
                continue
            whole = c0 + tk - 1 <= r0
            carries[a] = update(qs[a], kv, carries[a], None if whole else c0 - r0)
    for a in range(n_sub):
        _, l, acc = carries[a]
        o_ref[a * tr:(a + 1) * tr, :] = (acc / l).astype(o_ref.dtype)


def attention(q, k, v, *, tr, tk, n_sub):
    h, s, _ = q.shape
    tq = tr * n_sub
    assert tq % tk == 0
    exp2_scale = QK_DIM ** -0.5 * math.log2(math.e)
    return pl.pallas_call(
        functools.partial(_attention_kernel, tr=tr, tk=tk, n_sub=n_sub, exp2_scale=exp2_scale),
        grid=(h, s // tq),
        in_specs=[
            pl.BlockSpec((1, tq, QK_DIM), lambda hh, i: (hh, i, 0)),
            pl.BlockSpec((1, s, QK_DIM), lambda hh, i: (hh, 0, 0)),
            pl.BlockSpec((1, s, V_HEAD), lambda hh, i: (hh, 0, 0)),
        ],
        out_specs=pl.BlockSpec((tq, V_HEAD), lambda hh, i: (i, hh)),
        out_shape=jax.ShapeDtypeStruct((s, h * V_HEAD), BF16),
        compiler_params=_params("arbitrary", "arbitrary"),
        name="attention",
    )(q, k, v)


def _causal_conv_tile(x_refs, cw_ref, xbuf, tm):
    @pl.when(pl.program_id(0) == 0)
    def _():
        xbuf[0:SUBLANES, :] = jnp.zeros((SUBLANES, xbuf.shape[1]), F32)

    off = 0
    for x_ref in x_refs:
        xbuf[SUBLANES:SUBLANES + tm, off:off + x_ref.shape[1]] = x_ref[...]
        off += x_ref.shape[1]
    cw = cw_ref[...]
    acc = cw[0:1, :] * xbuf[pl.ds(SUBLANES - CONV_W + 1, tm), :]
    for k in range(1, CONV_W):
        acc = acc + cw[k:k + 1, :] * xbuf[pl.ds(SUBLANES - CONV_W + 1 + k, tm), :]
    xbuf[0:SUBLANES, :] = xbuf[tm:tm + SUBLANES, :]
    return acc


def _rglru_kernel(x_ref, y_ref, cw_ref, cb_ref, wg_ref, ba_ref, bx_ref, lam_ref, o_ref,
                  xbuf, a_s, b_s, h_carry, *, tm):
    @pl.when(pl.program_id(0) == 0)
    def _():
        h_carry[...] = jnp.zeros(h_carry.shape, F32)

    xc = _causal_conv_tile([x_ref], cw_ref, xbuf, tm) + cb_ref[...]
    xcb = xc.astype(BF16)
    pair_w = 2 * LRU_BLOCK_W
    r_parts, i_parts = [], []
    for p in range(LRU_BLOCKS // 2):
        g = _dot(xcb[:, p * pair_w:(p + 1) * pair_w], wg_ref[p])
        r_parts.append(g[:, :pair_w])
        i_parts.append(g[:, pair_w:])
    r = jax.nn.sigmoid(jnp.concatenate(r_parts, axis=-1) + ba_ref[...])
    gi = jax.nn.sigmoid(jnp.concatenate(i_parts, axis=-1) + bx_ref[...])
    log_a = -LRU_C * r * _softplus(-lam_ref[...])
    t = jnp.tanh(log_a)
    mult = jnp.sqrt(-2.0 * t / (1.0 - t))
    a_s[...] = jnp.exp(log_a)
    b_s[...] = mult * (gi * xc)

    row = lax.broadcasted_iota(jnp.int32, (SUBLANES, LRU_WIDTH), 0)

    def group(gidx, h_prev):
        r0 = pl.multiple_of(gidx * SUBLANES, SUBLANES)
        a = a_s[pl.ds(r0, SUBLANES), :]
        b = b_s[pl.ds(r0, SUBLANES), :]
        shift = 1
        while shift < SUBLANES:
            keep = row >= shift
            a_sh = jnp.where(keep, pltpu.roll(a, shift, 0), 1.0)
            b_sh = jnp.where(keep, pltpu.roll(b, shift, 0), 0.0)
            b = a * b_sh + b
            a = a * a_sh
            shift *= 2
        hs = a * h_prev + b
        b_s[pl.ds(r0, SUBLANES), :] = hs
        return hs[SUBLANES - 1:SUBLANES, :]

    h_carry[...] = lax.fori_loop(0, tm // SUBLANES, group, h_carry[...])
    o_ref[...] = (b_s[...] * _gelu_tanh(y_ref[...])).astype(o_ref.dtype)


def rglru(proj, x_col, y_col, conv_w, conv_b, w_gates, b_a, b_x, lam, layer, *, tm):
    s = proj.shape[0]
    full = lambda a: _per_layer(a, layer)
    return pl.pallas_call(
        functools.partial(_rglru_kernel, tm=tm),
        grid=(s // tm,),
        in_specs=[
            pl.BlockSpec((tm, LRU_WIDTH), lambda i: (i, x_col)),
            pl.BlockSpec((tm, LRU_WIDTH), lambda i: (i, y_col)),
            full(conv_w), full(conv_b), full(w_gates), full(b_a), full(b_x), full(lam),
        ],
        out_specs=pl.BlockSpec((tm, LRU_WIDTH), lambda i: (i, 0)),
        out_shape=jax.ShapeDtypeStruct((s, LRU_WIDTH), BF16),
        scratch_shapes=[
            pltpu.VMEM((tm + SUBLANES, LRU_WIDTH), F32),
            pltpu.VMEM((tm, LRU_WIDTH), F32),
            pltpu.VMEM((tm, LRU_WIDTH), F32),
            pltpu.VMEM((1, LRU_WIDTH), F32),
        ],
        compiler_params=_params("arbitrary"),
        name="rglru",
    )(proj, proj, conv_w, conv_b, w_gates, b_a, b_x, lam)


def _unit_lower_solve(l_mats, rhs, block_diag, cast, mm):
    heads = range(len(l_mats))
    d = [jnp.where(block_diag, l_mats[h], 0.0) for h in heads]
    n = [l_mats[h] - d[h] for h in heads]
    dc = [cast(d[h]) for h in heads]
    d2 = [mm(dc[h], dc[h]) for h in heads]
    d2c = [cast(d2[h]) for h in heads]
    d4 = [mm(d2c[h], d2c[h]) for h in heads]
    d4c = [cast(d4[h]) for h in heads]
    d8 = [mm(d4c[h], d4c[h]) for h in heads]
    d8c = [cast(d8[h]) for h in heads]
    x = [-d[h] for h in heads]
    for dp, dpc in ((d2, d2c), (d4, d4c), (d8, d8c)):
        x = [x[h] + dp[h] + mm(cast(x[h]), dpc[h]) for h in heads]
    xc = [cast(x[h]) for h in heads]
    m = [n[h] + mm(xc[h], cast(n[h])) for h in heads]
    mc = [cast(m[h]) for h in heads]
    m2 = [mm(mc[h], mc[h]) for h in heads]
    y = [m2[h] - m[h] - mm(mc[h], cast(m2[h])) for h in heads]
    r1 = [rhs[h] + mm(xc[h], cast(rhs[h])) for h in heads]
    return [r1[h] + mm(cast(y[h]), cast(r1[h])) for h in heads]


def _gdn_kernel(qin_ref, kin_ref, vin_ref, z_ref, ba_ref, cw_ref, alog_ref, dtb_ref, nw_ref, o_ref,
                xbuf, q_s, k_s, v_s, g_s, beta_s, u_s, wq_s, kt_s, qk_s, el_s, state,
                *, tm, chunks_per_step, exact_inverse):
    @pl.when(pl.program_id(0) == 0)
    def _():
        state[...] = jnp.zeros(state.shape, F32)

    act = _silu(_causal_conv_tile([qin_ref, kin_ref, vin_ref], cw_ref, xbuf, tm))
    kw = GDN_HEADS * GDN_DK
    for h in range(GDN_HEADS):
        sl = slice(h * GDN_DK, (h + 1) * GDN_DK)
        qh = act[:, h * GDN_DK:(h + 1) * GDN_DK]
        kh = act[:, kw + h * GDN_DK:kw + (h + 1) * GDN_DK]
        q_s[:, sl] = qh * lax.rsqrt(jnp.sum(qh * qh, axis=-1, keepdims=True) + EPS) * (GDN_DK ** -0.5)
        k_s[:, sl] = kh * lax.rsqrt(jnp.sum(kh * kh, axis=-1, keepdims=True) + EPS)
    v_s[...] = act[:, 2 * kw:]
    ba = ba_ref[...]
    beta_s[...] = jax.nn.sigmoid(ba[:, SMALL_B_OFF % LANES:SMALL_B_OFF % LANES + GDN_HEADS])
    a_logit = ba[:, SMALL_A_OFF % LANES:SMALL_A_OFF % LANES + GDN_HEADS]
    g_s[...] = -jnp.exp(alog_ref[...]) * _softplus(a_logit + dtb_ref[...])

    c = GDN_CHUNK
    row = lax.broadcasted_iota(jnp.int32, (c, c), 0)
    col = lax.broadcasted_iota(jnp.int32, (c, c), 1)
    incl = row >= col
    strict = row > col
    tril = incl.astype(F32)
    stril = strict.astype(F32)
    block_diag = jnp.logical_and(strict, (row // GDN_INV_BLOCK) == (col // GDN_INV_BLOCK))
    norm_w = nw_ref[...]
    nt_dims = (((1,), (1,)), ((), ()))
    tn_dims = (((0,), (0,)), ((), ()))
    heads = range(GDN_HEADS)
    hs = lambda h: slice(h * GDN_DK, (h + 1) * GDN_DK)
    cs = lambda h: slice(h * c, (h + 1) * c)
    bf = lambda t: t.astype(BF16)
    if exact_inverse:
        cast = lambda t: t
        mm = lambda a, b: _dot(a, b, HIGHEST)
    else:
        cast = bf
        mm = _dot

    def chunk_operands(ci):
        r0 = pl.multiple_of(ci * c, c)
        rows = pl.ds(r0, c)
        g_c = g_s[rows, :]
        beta_c = beta_s[rows, :]
        gc = _dot(tril, g_c, HIGHEST)
        gb = jnp.concatenate([jnp.broadcast_to(g_c[:, h:h + 1], (c, c)) * stril for h in heads], axis=1)
        diff = _dot(tril, gb, HIGHEST)
        e_gc = jnp.exp(gc)
        g_last = gc[c - 1:c, :]
        e_tail = jnp.exp(g_last - gc)
        el_s[pl.ds(ci, 1), :] = jnp.exp(g_last)
        q = [q_s[rows, hs(h)] for h in heads]
        k = [k_s[rows, hs(h)] for h in heads]
        bh = [jnp.broadcast_to(beta_c[:, h:h + 1], (c, GDN_DK)) for h in heads]
        eg = [jnp.broadcast_to(e_gc[:, h:h + 1], (c, GDN_DK)) for h in heads]
        kb = [k[h] * bh[h] for h in heads]
        decay = [jnp.where(incl, jnp.exp(diff[:, cs(h)]), 0.0) for h in heads]
        kq = [lax.dot_general(bf(jnp.concatenate([kb[h], q[h]], axis=0)), bf(k[h]), nt_dims,
                              preferred_element_type=F32) for h in heads]
        l_mats = [jnp.where(strict, kq[h][:c] * decay[h], 0.0) for h in heads]
        for h in heads:
            qk_s[rows, cs(h)] = bf(kq[h][c:] * decay[h])
            kt_s[rows, hs(h)] = bf(k[h] * jnp.broadcast_to(e_tail[:, h:h + 1], (c, GDN_DK)))
        rhs = [jnp.concatenate([v_s[rows, hs(h)] * bh[h], kb[h] * eg[h]], axis=1) for h in heads]
        q_dec = [q[h] * eg[h] for h in heads]
        return r0, l_mats, rhs, q_dec

    def prepare(step, carry):
        parts = [chunk_operands(step * chunks_per_step + t) for t in range(chunks_per_step)]
        l_mats = [l for part in parts for l in part[1]]
        rhs = [r for part in parts for r in part[2]]
        uw = _unit_lower_solve(l_mats, rhs, block_diag, cast, mm)
        for t, (r0, _, _, q_dec) in enumerate(parts):
            rows = pl.ds(r0, c)
            rows_w = pl.ds(pl.multiple_of(2 * r0, 2 * c), c)
            rows_q = pl.ds(pl.multiple_of(2 * r0 + c, c), c)
            for h in heads:
                uw_h = uw[t * GDN_HEADS + h]
                u_s[rows, hs(h)] = uw_h[:, :GDN_DV]
                wq_s[rows_w, hs(h)] = bf(uw_h[:, GDN_DV:])
                wq_s[rows_q, hs(h)] = bf(q_dec[h])
        return carry

    lax.fori_loop(0, tm // (c * chunks_per_step), prepare, 0)

    def advance(ci, carry):
        r0 = pl.multiple_of(ci * c, c)
        rows = pl.ds(r0, c)
        rows_wq = pl.ds(pl.multiple_of(2 * r0, 2 * c), 2 * c)
        e_last = el_s[pl.ds(ci, 1), :]
        s = [state[h] for h in heads]
        ws_qs = [_dot(wq_s[rows_wq, hs(h)], bf(s[h])) for h in heads]
        v_new = [bf(u_s[rows, hs(h)] - ws_qs[h][:c]) for h in heads]
        o = [ws_qs[h][c:] + _dot(qk_s[rows, cs(h)], v_new[h]) for h in heads]
        d_state = [lax.dot_general(kt_s[rows, hs(h)], v_new[h], tn_dims, preferred_element_type=F32)
                   for h in heads]
        for h in heads:
            state[h] = s[h] * e_last[:, h:h + 1] + d_state[h]
        for h in heads:
            o_ref[rows, hs(h)] = (_rms(o[h], norm_w) * _silu(z_ref[rows, hs(h)])).astype(o_ref.dtype)
        return carry

    lax.fori_loop(0, tm // c, advance, 0)


def gdn(proj, proj_small, q_col, ba_col, conv_w, a_log, dt_bias, norm_w, layer, *, tm, chunks_per_step,
        exact_inverse):
    s = proj.shape[0]
    width = GDN_HEADS * GDN_DV
    per_layer = lambda a: _per_layer(a, layer)
    col = lambda c: pl.BlockSpec((tm, width), lambda i: (i, c))
    return pl.pallas_call(
        functools.partial(_gdn_kernel, tm=tm, chunks_per_step=chunks_per_step,
                          exact_inverse=exact_inverse),
        grid=(s // tm,),
        in_specs=[
            col(q_col), col(q_col + 1), col(q_col + 2), col(q_col + 3),
            pl.BlockSpec((tm, LANES), lambda i: (i, ba_col)),
            per_layer(conv_w), per_layer(a_log), per_layer(dt_bias), per_layer(norm_w),
        ],
        out_specs=pl.BlockSpec((tm, width), lambda i: (i, 0)),
        out_shape=jax.ShapeDtypeStruct((s, width), BF16),
        scratch_shapes=[
            pltpu.VMEM((tm + SUBLANES, GDN_QKV_COLS), F32),
            pltpu.VMEM((tm, width), F32),
            pltpu.VMEM((tm, width), F32),
            pltpu.VMEM((tm, width), F32),
            pltpu.VMEM((tm, GDN_HEADS), F32),
            pltpu.VMEM((tm, GDN_HEADS), F32),
            pltpu.VMEM((tm, width), F32),
            pltpu.VMEM((2 * tm, width), BF16),
            pltpu.VMEM((tm, width), BF16),
            pltpu.VMEM((tm, GDN_HEADS * GDN_CHUNK), BF16),
            pltpu.VMEM((tm // GDN_CHUNK, GDN_HEADS), F32),
            pltpu.VMEM((GDN_HEADS, GDN_DK, GDN_DV), F32),
        ],
        compiler_params=_params("arbitrary"),
        name="gdn",
    )(proj, proj, proj, proj, proj_small, conv_w, a_log, dt_bias, norm_w)


def _merge_kernel(h_ref, ym_ref, yl_ref, yg_ref, wg0_ref, wg1_ref, wg2_ref, bg_ref, wb_ref, o_ref):
    bg = bg_ref[...]
    h = h_ref[...]
    acc = None
    for n, (y_ref, wg_ref) in enumerate(((ym_ref, wg0_ref), (yl_ref, wg1_ref), (yg_ref, wg2_ref))):
        gate = jax.nn.sigmoid(_dot(h, wg_ref[...]) + bg[n:n + 1, :])
        term = gate * _dot(y_ref[...], wb_ref[n])
        acc = term if acc is None else acc + term
    o_ref[...] = acc.astype(o_ref.dtype)


def merge(h, y_mla, y_lru, y_gdn, w_gate, b_gate, w_branch, layer, *, tm, tn):
    s, bw = y_mla.shape
    k = h.shape[1]
    d = w_branch.shape[3]
    nj = d // tn
    y_spec = pl.BlockSpec((tm, bw), lambda i, j: (i, 0))
    gate_spec = lambda n: pl.BlockSpec((None, k, tn), lambda i, j: (layer, 0, n * nj + j))
    return pl.pallas_call(
        _merge_kernel,
        grid=(s // tm, nj),
        in_specs=[pl.BlockSpec((tm, k), lambda i, j: (i, 0)), y_spec, y_spec, y_spec,
                  gate_spec(0), gate_spec(1), gate_spec(2),
                  pl.BlockSpec((None, N_BRANCH, tn), lambda i, j: (layer, 0, j)),
                  pl.BlockSpec((None, N_BRANCH, bw, tn), lambda i, j: (layer, 0, 0, j))],
        out_specs=pl.BlockSpec((tm, tn), lambda i, j: (i, j)),
        out_shape=jax.ShapeDtypeStruct((s, d), BF16),
        compiler_params=_params("arbitrary", "arbitrary"),
        name="merge",
    )(h, y_mla, y_lru, y_gdn, w_gate, w_gate, w_gate, b_gate, w_branch)


def _residual_matmul_kernel(x_ref, a_ref, w_ref, o_ref):
    o_ref[...] = x_ref[...] + _dot(a_ref[...], w_ref[...])


def residual_matmul(x, a, w, layer, *, tm, tn):
    s, k = a.shape
    n = w.shape[2]
    return pl.pallas_call(
        _residual_matmul_kernel,
        grid=(s // tm, n // tn),
        in_specs=[
            pl.BlockSpec((tm, tn), lambda i, j: (i, j)),
            pl.BlockSpec((tm, k), lambda i, j: (i, 0)),
            pl.BlockSpec((None, k, tn), lambda i, j: (layer, 0, j)),
        ],
        out_specs=pl.BlockSpec((tm, tn), lambda i, j: (i, j)),
        out_shape=jax.ShapeDtypeStruct((s, n), F32),
        compiler_params=_params("arbitrary", "arbitrary"),
        name="residual_matmul",
    )(x, a, w)


def _ffn_kernel(x_ref, g_ref, wg_ref, wu_ref, wd_ref, fg_ref, o_ref, h_ref, *, final_norm):
    j = pl.program_id(1)

    @pl.when(j == 0)
    def _():
        x = x_ref[...]
        h_ref[...] = _rms(x, g_ref[...]).astype(BF16)
        o_ref[...] = x

    h = h_ref[...]
    a = _silu(_dot(h, wg_ref[...])) * _dot(h, wu_ref[...])
    o_ref[...] += _dot(a.astype(BF16), wd_ref[...])

    if final_norm:
        @pl.when(j == pl.num_programs(1) - 1)
        def _():
            o_ref[...] = _rms(o_ref[...], fg_ref[...])


def ffn(x, gain, w_gate, w_up, w_down, final_gain, layer, *, tm, th, final_norm):
    s, d = x.shape
    hidden = w_gate.shape[2]
    return pl.pallas_call(
        functools.partial(_ffn_kernel, final_norm=final_norm),
        grid=(s // tm, hidden // th),
        in_specs=[
            pl.BlockSpec((tm, d), lambda i, j: (i, 0)),
            _per_layer(gain, layer),
            pl.BlockSpec((None, d, th), lambda i, j: (layer, 0, j)),
            pl.BlockSpec((None, d, th), lambda i, j: (layer, 0, j)),
            pl.BlockSpec((None, th, d), lambda i, j: (layer, j, 0)),
            pl.BlockSpec((1, d), lambda i, j: (0, 0)),
        ],
        out_specs=pl.BlockSpec((tm, d), lambda i, j: (i, 0)),
        out_shape=jax.ShapeDtypeStruct((s, d), F32),
        scratch_shapes=[pltpu.VMEM((tm, d), BF16)],
        compiler_params=_params("arbitrary", "arbitrary"),
        name="ffn",
    )(x, gain, w_gate, w_up, w_down, final_gain)


W_IN_MAIN_OFF = SMALL_B_OFF
W_IN_MAIN_COLS = 2 * LRU_WIDTH + 2 * GDN_HEADS * GDN_DK + 2 * GDN_HEADS * GDN_DV
W_IN_BA_OFF = W_IN_MAIN_OFF + W_IN_MAIN_COLS
W_IN_GATE_OFF = W_IN_BA_OFF + 2 * GDN_HEADS
W_IN_COLS = W_IN_GATE_OFF + GATE_COLS


def _lane_window(ref, start, size):
    base = start // LANES * LANES
    stop = min(-(-(start + size) // LANES) * LANES, ref.shape[1])
    return ref[:, base:stop][:, start - base:start - base + size]


def _regroup_w_in_kernel(w_ref, small_ref, main_ref, gate_ref):
    rows = w_ref.shape[0]
    pad = jnp.zeros((rows, SMALL_COLS - SMALL_A_OFF - GDN_HEADS), F32)
    small = jnp.concatenate([w_ref[:, :W_IN_MAIN_OFF], _lane_window(w_ref, W_IN_BA_OFF, 2 * GDN_HEADS), pad],
                            axis=1)
    small_ref[...] = small.astype(BF16)
    main_ref[...] = _lane_window(w_ref, W_IN_MAIN_OFF, W_IN_MAIN_COLS).astype(BF16)
    gate_ref[...] = _lane_window(w_ref, W_IN_GATE_OFF, GATE_COLS).astype(BF16)


def regroup_w_in(w_in, *, tk):
    depth, d, n = w_in.shape
    assert n == W_IN_COLS
    out = lambda cols: pl.BlockSpec((None, tk, cols), lambda l, i: (l, i, 0))
    return pl.pallas_call(
        _regroup_w_in_kernel,
        grid=(depth, d // tk),
        in_specs=[pl.BlockSpec((None, tk, n), lambda l, i: (l, i, 0))],
        out_specs=[out(SMALL_COLS), out(W_IN_MAIN_COLS), out(GATE_COLS)],
        out_shape=[jax.ShapeDtypeStruct((depth, d, SMALL_COLS), BF16),
                   jax.ShapeDtypeStruct((depth, d, W_IN_MAIN_COLS), BF16),
                   jax.ShapeDtypeStruct((depth, d, GATE_COLS), BF16)],
        compiler_params=_params("arbitrary", "arbitrary"),
        name="regroup_w_in",
    )(w_in)


def _regroup_weights(w_in, w_uq, w_ukv, lru_w_a, lru_w_x):
    depth = w_in.shape[0]
    w_small, w_main, w_gate = regroup_w_in(w_in, tk=256)

    uq = w_uq.reshape(depth, Q_LORA, MLA_HEADS, QK_DIM)
    flat = lambda t: t.reshape(depth, t.shape[1], -1).astype(BF16)
    wqn = flat(uq[..., :QK_NOPE])
    wq1 = flat(uq[..., QK_NOPE:QK_NOPE + ROPE_HALF])
    wq2 = flat(uq[..., QK_NOPE + ROPE_HALF:])
    ukv = w_ukv.reshape(depth, KV_LORA, MLA_HEADS, QK_NOPE + V_HEAD)
    wk = flat(ukv[..., :QK_NOPE])
    wv = flat(ukv[..., QK_NOPE:])

    def pair_tiles(w):
        pairs = w.reshape(depth, LRU_BLOCKS // 2, 2, LRU_BLOCK_W, LRU_BLOCK_W)
        z = jnp.zeros_like(pairs[:, :, 0])
        top = jnp.concatenate([pairs[:, :, 0], z], axis=-1)
        bottom = jnp.concatenate([z, pairs[:, :, 1]], axis=-1)
        return jnp.concatenate([top, bottom], axis=-2)

    w_gates = jnp.concatenate([pair_tiles(lru_w_a), pair_tiles(lru_w_x)], axis=-1).astype(BF16)
    return w_main, w_gate, w_small, wqn, wq1, wq2, wk, wv, w_gates


def _rows(v):
    return v.reshape(v.shape[0], 1, -1)


def kernel(x, positions, norm_mix, w_in, mla_q_norm, mla_w_uq, mla_kv_norm, mla_w_ukv, lru_conv_w, lru_conv_b, lru_w_a, lru_b_a, lru_w_x, lru_b_x, lru_lambda, gdn_conv_w, gdn_a_log, gdn_dt_bias, gdn_norm, w_branch, b_gate, w_out, norm_ffn, ffn_w_gate, ffn_w_up, ffn_w_down, norm_final):
    b, s, d = x.shape
    assert b == 1
    depth = w_in.shape[0]
    xs = x.reshape(s, d)

    half = ROPE_HALF
    inv_freq = ROPE_THETA ** (-jnp.arange(half, dtype=F32) / half)
    inv_b = jnp.tile(inv_freq, MLA_HEADS).reshape(1, MLA_HEADS * half)
    pos_b = jnp.broadcast_to(positions.astype(F32).reshape(s, 1), (s, MLA_HEADS * half))
    cos, sin = rope_tables(pos_b, inv_b, tm=1024)

    lru_x_col, gdn_q_col = 0, 2
    ba_col = SMALL_ROPE_OFF // LANES

    w_main, w_gate, w_small, wqn, wq1, wq2, wk, wv, w_gates = _regroup_weights(
        w_in, mla_w_uq, mla_w_ukv, lru_w_a, lru_w_x)
    w_branch_b = w_branch.astype(BF16)
    w_out_b = w_out.astype(BF16)
    ffn_gate_b = ffn_w_gate.astype(BF16)
    ffn_up_b = ffn_w_up.astype(BF16)
    ffn_down_b = ffn_w_down.astype(BF16)
    gain_mix, gain_ffn = _rows(norm_mix), _rows(norm_ffn)
    gain_q, gain_kv = _rows(mla_q_norm), _rows(mla_kv_norm)
    lru_cb, lru_ba, lru_bx, lru_lam = _rows(lru_conv_b), _rows(lru_b_a), _rows(lru_b_x), _rows(lru_lambda)
    gdn_alog, gdn_dtb, gdn_nw = _rows(gdn_a_log), _rows(gdn_dt_bias), _rows(gdn_norm)
    gain_final = norm_final.reshape(1, d)

    for l in range(depth):
        proj_small, h = norm_matmul(xs, gain_mix, w_small, l, tm=1024, tn=SMALL_COLS)
        proj = matmul(h, w_main, l, tm=MAIN_TM, tn=MAIN_TN)

        q, k, v = mla_prep(proj_small, gain_q, gain_kv, wqn, wq1, wq2, wk, wv, cos, sin, l, tm=512)
        y_mla = attention(q, k, v, tr=1024, tk=1024, n_sub=2)
        y_lru = rglru(proj, lru_x_col, lru_x_col + 1, lru_conv_w, lru_cb, w_gates,
                      lru_ba, lru_bx, lru_lam, l, tm=512)
        y_gdn = gdn(proj, proj_small, gdn_q_col, ba_col, gdn_conv_w, gdn_alog, gdn_dtb, gdn_nw, l,
                    tm=512, chunks_per_step=2, exact_inverse=False)

        mixed = merge(h, y_mla, y_lru, y_gdn, w_gate, b_gate, w_branch_b, l, tm=MERGE_TM, tn=MERGE_TN)
        xs = residual_matmul(xs, mixed, w_out_b, l, tm=OUT_TM, tn=OUT_TN)
        xs = ffn(xs, gain_ffn, ffn_gate_b, ffn_up_b, ffn_down_b, gain_final, l, tm=FFN_TM, th=FFN_TH,
                 final_norm=(l == depth - 1))
    return xs.reshape(b, s, d)
```

```python
import functools
import math

import jax
import jax.numpy as jnp
from jax import lax
from jax.experimental import pallas as pl
from jax.experimental.pallas import tpu as pltpu

F32 = jnp.float32
BF16 = jnp.bfloat16
HIGHEST = lax.Precision.HIGHEST

EPS = 1e-6
N_BRANCH = 3
MLA_HEADS = 8
QK_NOPE = 128
QK_ROPE = 64
ROPE_HALF = QK_ROPE // 2
QK_DIM = QK_NOPE + QK_ROPE
V_HEAD = 128
Q_LORA = 512
KV_LORA = 512
ROPE_THETA = 10000.0
LRU_WIDTH = 1024
LRU_BLOCKS = 8
LRU_BLOCK_W = LRU_WIDTH // LRU_BLOCKS
LRU_C = 8.0
CONV_W = 4
GDN_HEADS = 8
GDN_DK = 128
GDN_DV = 128
GDN_CHUNK = 64
GDN_INV_BLOCK = 16

V7X_VMEM_BYTES = 64 * 1024 * 1024
VMEM_LIMIT_BYTES = V7X_VMEM_BYTES - 8 * 1024 * 1024
SUBLANES = 8
LANES = 128

MAIN_TM, MAIN_TN = 2048, 1024
MERGE_TM, MERGE_TN = 1024, 512
OUT_TM, OUT_TN = 1024, 1024
FFN_TM, FFN_TH = 512, 512

GATE_COLS = N_BRANCH * 2048
GDN_QKV_COLS = 3 * GDN_HEADS * GDN_DK
SMALL_COLS = 1152
SMALL_ROPE_OFF = Q_LORA + KV_LORA
SMALL_B_OFF = SMALL_ROPE_OFF + QK_ROPE
SMALL_A_OFF = SMALL_B_OFF + GDN_HEADS


def _params(*semantics):
    return pltpu.CompilerParams(dimension_semantics=semantics, vmem_limit_bytes=VMEM_LIMIT_BYTES)


def _dot(a, b, precision=None):
    return jnp.dot(a, b, preferred_element_type=F32, precision=precision)


def _rms(x, gain):
    return x * lax.rsqrt(jnp.mean(x * x, axis=-1, keepdims=True) + EPS) * gain


def _softplus(x):
    return jnp.maximum(x, 0.0) + jnp.log1p(jnp.exp(-jnp.abs(x)))


def _silu(x):
    return x * jax.nn.sigmoid(x)


def _gelu_tanh(x):
    return 0.5 * x * (1.0 + jnp.tanh(math.sqrt(2.0 / math.pi) * (x + 0.044715 * (x * x * x))))


def _norm_matmul_kernel(x_ref, g_ref, w_ref, o_ref, h_ref):
    @pl.when(pl.program_id(1) == 0)
    def _():
        h_ref[...] = _rms(x_ref[...], g_ref[...]).astype(BF16)

    o_ref[...] = _dot(h_ref[...], w_ref[...]).astype(o_ref.dtype)


def _per_layer(a, layer):
    return pl.BlockSpec((None,) + a.shape[1:], lambda *_: (layer,) + (0,) * (a.ndim - 1))


def norm_matmul(x, gain, w, layer, *, tm, tn):
    s, k = x.shape
    n = w.shape[2]
    return pl.pallas_call(
        _norm_matmul_kernel,
        grid=(s // tm, n // tn),
        in_specs=[
            pl.BlockSpec((tm, k), lambda i, j: (i, 0)),
            _per_layer(gain, layer),
            pl.BlockSpec((None, k, tn), lambda i, j: (layer, 0, j)),
        ],
        out_specs=[pl.BlockSpec((tm, tn), lambda i, j: (i, j)),
                   pl.BlockSpec((tm, k), lambda i, j: (i, 0))],
        out_shape=[jax.ShapeDtypeStruct((s, n), F32), jax.ShapeDtypeStruct((s, k), BF16)],
        compiler_params=_params("arbitrary", "arbitrary"),
        name="norm_matmul",
    )(x, gain, w)


def _matmul_kernel(a_ref, w_ref, o_ref):
    o_ref[...] = _dot(a_ref[...], w_ref[...])


def matmul(a, w, layer, *, tm, tn):
    s, k = a.shape
    n = w.shape[2]
    return pl.pallas_call(
        _matmul_kernel,
        grid=(s // tm, n // tn),
        in_specs=[
            pl.BlockSpec((tm, k), lambda i, j: (i, 0)),
            pl.BlockSpec((None, k, tn), lambda i, j: (layer, 0, j)),
        ],
        out_specs=pl.BlockSpec((tm, tn), lambda i, j: (i, j)),
        out_shape=jax.ShapeDtypeStruct((s, n), F32),
        compiler_params=_params("arbitrary", "arbitrary"),
        name="matmul",
    )(a, w)


def _rope_table_kernel(pos_ref, inv_ref, cos_ref, sin_ref):
    ang = pos_ref[...] * inv_ref[...]
    cos_ref[...] = jnp.cos(ang)
    sin_ref[...] = jnp.sin(ang)


def rope_tables(pos_b, inv_b, *, tm):
    s, n = pos_b.shape
    blk = pl.BlockSpec((tm, n), lambda i: (i, 0))
    return pl.pallas_call(
        _rope_table_kernel,
        grid=(s // tm,),
        in_specs=[blk, pl.BlockSpec((1, n), lambda i: (0, 0))],
        out_specs=[blk, blk],
        out_shape=[jax.ShapeDtypeStruct((s, n), F32)] * 2,
        compiler_params=_params("arbitrary"),
        name="rope_tables",
    )(pos_b, inv_b)


def _mla_prep_kernel(p_ref, qg_ref, kvg_ref, wqn_ref, wq1_ref, wq2_ref, wk_ref, wv_ref,
                     cos_ref, sin_ref, q_out, k_out, v_out):
    p = p_ref[...]
    hq = _rms(p[:, :Q_LORA], qg_ref[...]).astype(BF16)
    hkv = _rms(p[:, Q_LORA:SMALL_ROPE_OFF], kvg_ref[...]).astype(BF16)
    kr = p[:, SMALL_ROPE_OFF:SMALL_ROPE_OFF + QK_ROPE]
    cos = cos_ref[...]
    sin = sin_ref[...]
    qn = _dot(hq, wqn_ref[...])
    q1 = _dot(hq, wq1_ref[...])
    q2 = _dot(hq, wq2_ref[...])
    qr1 = q1 * cos - q2 * sin
    qr2 = q2 * cos + q1 * sin
    kn = _dot(hkv, wk_ref[...])
    v = _dot(hkv, wv_ref[...])
    c32 = cos[:, :ROPE_HALF]
    s32 = sin[:, :ROPE_HALF]
    k1 = kr[:, :ROPE_HALF]
    k2 = kr[:, ROPE_HALF:]
    kr1 = k1 * c32 - k2 * s32
    kr2 = k2 * c32 + k1 * s32
    for h in range(MLA_HEADS):
        nope = slice(h * QK_NOPE, (h + 1) * QK_NOPE)
        rope = slice(h * ROPE_HALF, (h + 1) * ROPE_HALF)
        q_out[h] = jnp.concatenate([qn[:, nope], qr1[:, rope], qr2[:, rope]], axis=-1).astype(BF16)
        k_out[h] = jnp.concatenate([kn[:, nope], kr1, kr2], axis=-1).astype(BF16)
        v_out[h] = v[:, h * V_HEAD:(h + 1) * V_HEAD].astype(BF16)


def mla_prep(proj_small, q_gain, kv_gain, wqn, wq1, wq2, wk, wv, cos, sin, layer, *, tm):
    s = proj_small.shape[0]
    full = lambda a: _per_layer(a, layer)
    rows = lambda n: pl.BlockSpec((tm, n), lambda i: (i, 0))
    heads = lambda d: pl.BlockSpec((MLA_HEADS, tm, d), lambda i: (0, i, 0))
    return pl.pallas_call(
        _mla_prep_kernel,
        grid=(s // tm,),
        in_specs=[rows(SMALL_COLS), full(q_gain), full(kv_gain), full(wqn), full(wq1), full(wq2),
                  full(wk), full(wv), rows(cos.shape[1]), rows(sin.shape[1])],
        out_specs=[heads(QK_DIM), heads(QK_DIM), heads(V_HEAD)],
        out_shape=[jax.ShapeDtypeStruct((MLA_HEADS, s, QK_DIM), BF16),
                   jax.ShapeDtypeStruct((MLA_HEADS, s, QK_DIM), BF16),
                   jax.ShapeDtypeStruct((MLA_HEADS, s, V_HEAD), BF16)],
        compiler_params=_params("arbitrary"),
        name="mla_prep",
    )(proj_small, q_gain, kv_gain, wqn, wq1, wq2, wk, wv, cos, sin)


def _attention_kernel(q_ref, k_ref, v_ref, o_ref, *, tr, tk, n_sub, exp2_scale):
    i = pl.program_id(1)
    tq = tr * n_sub
    qs = [q_ref[0, a * tr:(a + 1) * tr, :] for a in range(n_sub)]

    def kv_block(j):
        start = pl.multiple_of(j * tk, tk)
        return k_ref[0, pl.ds(start, tk), :], v_ref[0, pl.ds(start, tk), :]

    def update(q, kv, carry, mask_offset):
        m, l, acc = carry
        kj, vj = kv
        s = lax.dot_general(q, kj, (((1,), (1,)), ((), ())), preferred_element_type=F32)
        if mask_offset is not None:
            row = lax.broadcasted_iota(jnp.int32, (tr, tk), 0)
            col = lax.broadcasted_iota(jnp.int32, (tr, tk), 1)
            s = jnp.where(col + mask_offset <= row, s, -jnp.inf)
        m_new = jnp.maximum(m, jnp.max(s, axis=-1, keepdims=True))
        p = jnp.exp2((s - m_new) * exp2_scale)
        alpha = jnp.exp2((m - m_new) * exp2_scale)
        l = alpha * l + jnp.sum(p, axis=-1, keepdims=True)
        acc = alpha * acc + _dot(p.astype(BF16), vj)
        return m_new, l, acc

    def full_blocks(j, carries):
        kv = kv_block(j)
        return tuple(update(qs[a], kv, carries[a], None) for a in range(n_sub))

    init = (jnp.full((tr, 1), -jnp.inf, F32), jnp.zeros((tr, 1), F32), jnp.zeros((tr, V_HEAD), F32))
    n_past = i * (tq // tk)
    carries = list(lax.fori_loop(0, n_past, full_blocks, (init,) * n_sub))
    for b in range(tq // tk):
        kv = kv_block(n_past + b)
        c0 = b * tk
        for a in range(n_sub):
            r0 = a * tr
            if c0 > r0 + tr - 1:
                continue
            whole = c0 + tk - 1 <= r0
            carries[a] = update(qs[a], kv, carries[a], None if whole else c0 - r0)
    for a in range(n_sub):
        _, l, acc = carries[a]
        o_ref[a * tr:(a + 1) * tr, :] = (acc / l).astype(o_ref.dtype)


def attention(q, k, v, *, tr, tk, n_sub):
    h, s, _ = q.shape
    tq = tr * n_sub
    assert tq % tk == 0
    exp2_scale = QK_DIM ** -0.5 * math.log2(math.e)
    return pl.pallas_call(
        functools.partial(_attention_kernel, tr=tr, tk=tk, n_sub=n_sub, exp2_scale=exp2_scale),
        grid=(h, s // tq),
        in_specs=[
            pl.BlockSpec((1, tq, QK_DIM), lambda hh, i: (hh, i, 0)),
            pl.BlockSpec((1, s, QK_DIM), lambda hh, i: (hh, 0, 0)),
            pl.BlockSpec((1, s, V_HEAD), lambda hh, i: (hh, 0, 0)),
        ],
        out_specs=pl.BlockSpec((tq, V_HEAD), lambda hh, i: (i, hh)),
        out_shape=jax.ShapeDtypeStruct((s, h * V_HEAD), BF16),
        compiler_params=_params("arbitrary", "arbitrary"),
        name="attention",
    )(q, k, v)


def _causal_conv_tile(x_refs, cw_ref, xbuf, tm):
    @pl.when(pl.program_id(0) == 0)
    def _():
        xbuf[0:SUBLANES, :] = jnp.zeros((SUBLANES, xbuf.shape[1]), F32)

    off = 0
    for x_ref in x_refs:
        xbuf[SUBLANES:SUBLANES + tm, off:off + x_ref.shape[1]] = x_ref[...]
        off += x_ref.shape[1]
    cw = cw_ref[...]
    acc = cw[0:1, :] * xbuf[pl.ds(SUBLANES - CONV_W + 1, tm), :]
    for k in range(1, CONV_W):
        acc = acc + cw[k:k + 1, :] * xbuf[pl.ds(SUBLANES - CONV_W + 1 + k, tm), :]
    xbuf[0:SUBLANES, :] = xbuf[tm:tm + SUBLANES, :]
    return acc


def _rglru_kernel(x_ref, y_ref, cw_ref, cb_ref, wg_ref, ba_ref, bx_ref, lam_ref, o_ref,
                  xbuf, a_s, b_s, h_carry, *, tm):
    @pl.when(pl.program_id(0) == 0)
    def _():
        h_carry[...] = jnp.zeros(h_carry.shape, F32)

    xc = _causal_conv_tile([x_ref], cw_ref, xbuf, tm) + cb_ref[...]
    xcb = xc.astype(BF16)
    pair_w = 2 * LRU_BLOCK_W
    r_parts, i_parts = [], []
    for p in range(LRU_BLOCKS // 2):
        g = _dot(xcb[:, p * pair_w:(p + 1) * pair_w], wg_ref[p])
        r_parts.append(g[:, :pair_w])
        i_parts.append(g[:, pair_w:])
    r = jax.nn.sigmoid(jnp.concatenate(r_parts, axis=-1) + ba_ref[...])
    gi = jax.nn.sigmoid(jnp.concatenate(i_parts, axis=-1) + bx_ref[...])
    log_a = -LRU_C * r * _softplus(-lam_ref[...])
    t = jnp.tanh(log_a)
    mult = jnp.sqrt(-2.0 * t / (1.0 - t))
    a_s[...] = jnp.exp(log_a)
    b_s[...] = mult * (gi * xc)

    row = lax.broadcasted_iota(jnp.int32, (SUBLANES, LRU_WIDTH), 0)

    def group(gidx, h_prev):
        r0 = pl.multiple_of(gidx * SUBLANES, SUBLANES)
        a = a_s[pl.ds(r0, SUBLANES), :]
        b = b_s[pl.ds(r0, SUBLANES), :]
        shift = 1
        while shift < SUBLANES:
            keep = row >= shift
            a_sh = jnp.where(keep, pltpu.roll(a, shift, 0), 1.0)
            b_sh = jnp.where(keep, pltpu.roll(b, shift, 0), 0.0)
            b = a * b_sh + b
            a = a * a_sh
            shift *= 2
        hs = a * h_prev + b
        b_s[pl.ds(r0, SUBLANES), :] = hs
        return hs[SUBLANES - 1:SUBLANES, :]

    h_carry[...] = lax.fori_loop(0, tm // SUBLANES, group, h_carry[...])
    o_ref[...] = (b_s[...] * _gelu_tanh(y_ref[...])).astype(o_ref.dtype)


def rglru(proj, x_col, y_col, conv_w, conv_b, w_gates, b_a, b_x, lam, layer, *, tm):
    s = proj.shape[0]
    full = lambda a: _per_layer(a, layer)
    return pl.pallas_call(
        functools.partial(_rglru_kernel, tm=tm),
        grid=(s // tm,),
        in_specs=[
            pl.BlockSpec((tm, LRU_WIDTH), lambda i: (i, x_col)),
            pl.BlockSpec((tm, LRU_WIDTH), lambda i: (i, y_col)),
            full(conv_w), full(conv_b), full(w_gates), full(b_a), full(b_x), full(lam),
        ],
        out_specs=pl.BlockSpec((tm, LRU_WIDTH), lambda i: (i, 0)),
        out_shape=jax.ShapeDtypeStruct((s, LRU_WIDTH), BF16),
        scratch_shapes=[
            pltpu.VMEM((tm + SUBLANES, LRU_WIDTH), F32),
            pltpu.VMEM((tm, LRU_WIDTH), F32),
            pltpu.VMEM((tm, LRU_WIDTH), F32),
            pltpu.VMEM((1, LRU_WIDTH), F32),
        ],
        compiler_params=_params("arbitrary"),
        name="rglru",
    )(proj, proj, conv_w, conv_b, w_gates, b_a, b_x, lam)


def _unit_lower_solve(l_mats, rhs, block_diag, cast, mm):
    heads = range(len(l_mats))
    d = [jnp.where(block_diag, l_mats[h], 0.0) for h in heads]
    n = [l_mats[h] - d[h] for h in heads]
    dc = [cast(d[h]) for h in heads]
    d2 = [mm(dc[h], dc[h]) for h in heads]
    d2c = [cast(d2[h]) for h in heads]
    d4 = [mm(d2c[h], d2c[h]) for h in heads]
    d4c = [cast(d4[h]) for h in heads]
    d8 = [mm(d4c[h], d4c[h]) for h in heads]
    d8c = [cast(d8[h]) for h in heads]
    x = [-d[h] for h in heads]
    for dp, dpc in ((d2, d2c), (d4, d4c), (d8, d8c)):
        x = [x[h] + dp[h] + mm(cast(x[h]), dpc[h]) for h in heads]
    xc = [cast(x[h]) for h in heads]
    m = [n[h] + mm(xc[h], cast(n[h])) for h in heads]
    mc = [cast(m[h]) for h in heads]
    m2 = [mm(mc[h], mc[h]) for h in heads]
    y = [m2[h] - m[h] - mm(mc[h], cast(m2[h])) for h in heads]
    r1 = [rhs[h] + mm(xc[h], cast(rhs[h])) for h in heads]
    return [r1[h] + mm(cast(y[h]), cast(r1[h])) for h in heads]


def _gdn_kernel(qin_ref, kin_ref, vin_ref, z_ref, ba_ref, cw_ref, alog_ref, dtb_ref, nw_ref, o_ref,
                xbuf, q_s, k_s, v_s, g_s, beta_s, u_s, wq_s, kt_s, qk_s, el_s, state,
                *, tm, chunks_per_step, exact_inverse):
    @pl.when(pl.program_id(0) == 0)
    def _():
        state[...] = jnp.zeros(state.shape, F32)

    act = _silu(_causal_conv_tile([qin_ref, kin_ref, vin_ref], cw_ref, xbuf, tm))
    kw = GDN_HEADS * GDN_DK
    for h in range(GDN_HEADS):
        sl = slice(h * GDN_DK, (h + 1) * GDN_DK)
        qh = act[:, h * GDN_DK:(h + 1) * GDN_DK]
        kh = act[:, kw + h * GDN_DK:kw + (h + 1) * GDN_DK]
        q_s[:, sl] = qh * lax.rsqrt(jnp.sum(qh * qh, axis=-1, keepdims=True) + EPS) * (GDN_DK ** -0.5)
        k_s[:, sl] = kh * lax.rsqrt(jnp.sum(kh * kh, axis=-1, keepdims=True) + EPS)
    v_s[...] = act[:, 2 * kw:]
    ba = ba_ref[...]
    beta_s[...] = jax.nn.sigmoid(ba[:, SMALL_B_OFF % LANES:SMALL_B_OFF % LANES + GDN_HEADS])
    a_logit = ba[:, SMALL_A_OFF % LANES:SMALL_A_OFF % LANES + GDN_HEADS]
    g_s[...] = -jnp.exp(alog_ref[...]) * _softplus(a_logit + dtb_ref[...])

    c = GDN_CHUNK
    row = lax.broadcasted_iota(jnp.int32, (c, c), 0)
    col = lax.broadcasted_iota(jnp.int32, (c, c), 1)
    incl = row >= col
    strict = row > col
    tril = incl.astype(F32)
    stril = strict.astype(F32)
    block_diag = jnp.logical_and(strict, (row // GDN_INV_BLOCK) == (col // GDN_INV_BLOCK))
    norm_w = nw_ref[...]
    nt_dims = (((1,), (1,)), ((), ()))
    tn_dims = (((0,), (0,)), ((), ()))
    heads = range(GDN_HEADS)
    hs = lambda h: slice(h * GDN_DK, (h + 1) * GDN_DK)
    cs = lambda h: slice(h * c, (h + 1) * c)
    bf = lambda t: t.astype(BF16)
    if exact_inverse:
        cast = lambda t: t
        mm = lambda a, b: _dot(a, b, HIGHEST)
    else:
        cast = bf
        mm = _dot

    def chunk_operands(ci):
        r0 = pl.multiple_of(ci * c, c)
        rows = pl.ds(r0, c)
        g_c = g_s[rows, :]
        beta_c = beta_s[rows, :]
        gc = _dot(tril, g_c, HIGHEST)
        gb = jnp.concatenate([jnp.broadcast_to(g_c[:, h:h + 1], (c, c)) * stril for h in heads], axis=1)
        diff = _dot(tril, gb, HIGHEST)
        e_gc = jnp.exp(gc)
        g_last = gc[c - 1:c, :]
        e_tail = jnp.exp(g_last - gc)
        el_s[pl.ds(ci, 1), :] = jnp.exp(g_last)
        q = [q_s[rows, hs(h)] for h in heads]
        k = [k_s[rows, hs(h)] for h in heads]
        bh = [jnp.broadcast_to(beta_c[:, h:h + 1], (c, GDN_DK)) for h in heads]
        eg = [jnp.broadcast_to(e_gc[:, h:h + 1], (c, GDN_DK)) for h in heads]
        kb = [k[h] * bh[h] for h in heads]
        decay = [jnp.where(incl, jnp.exp(diff[:, cs(h)]), 0.0) for h in heads]
        kq = [lax.dot_general(bf(jnp.concatenate([kb[h], q[h]], axis=0)), bf(k[h]), nt_dims,
                              preferred_element_type=F32) for h in heads]
        l_mats = [jnp.where(strict, kq[h][:c] * decay[h], 0.0) for h in heads]
        for h in heads:
            qk_s[rows, cs(h)] = bf(kq[h][c:] * decay[h])
            kt_s[rows, hs(h)] = bf(k[h] * jnp.broadcast_to(e_tail[:, h:h + 1], (c, GDN_DK)))
        rhs = [jnp.concatenate([v_s[rows, hs(h)] * bh[h], kb[h] * eg[h]], axis=1) for h in heads]
        q_dec = [q[h] * eg[h] for h in heads]
        return r0, l_mats, rhs, q_dec

    def prepare(step, carry):
        parts = [chunk_operands(step * chunks_per_step + t) for t in range(chunks_per_step)]
        l_mats = [l for part in parts for l in part[1]]
        rhs = [r for part in parts for r in part[2]]
        uw = _unit_lower_solve(l_mats, rhs, block_diag, cast, mm)
        for t, (r0, _, _, q_dec) in enumerate(parts):
            rows = pl.ds(r0, c)
            rows_w = pl.ds(pl.multiple_of(2 * r0, 2 * c), c)
            rows_q = pl.ds(pl.multiple_of(2 * r0 + c, c), c)
            for h in heads:
                uw_h = uw[t * GDN_HEADS + h]
                u_s[rows, hs(h)] = uw_h[:, :GDN_DV]
                wq_s[rows_w, hs(h)] = bf(uw_h[:, GDN_DV:])
                wq_s[rows_q, hs(h)] = bf(q_dec[h])
        return carry

    lax.fori_loop(0, tm // (c * chunks_per_step), prepare, 0)

    def advance(ci, carry):
        r0 = pl.multiple_of(ci * c, c)
        rows = pl.ds(r0, c)
        rows_wq = pl.ds(pl.multiple_of(2 * r0, 2 * c), 2 * c)
        e_last = el_s[pl.ds(ci, 1), :]
        s = [state[h] for h in heads]
        ws_qs = [_dot(wq_s[rows_wq, hs(h)], bf(s[h])) for h in heads]
        v_new = [bf(u_s[rows, hs(h)] - ws_qs[h][:c]) for h in heads]
        o = [ws_qs[h][c:] + _dot(qk_s[rows, cs(h)], v_new[h]) for h in heads]
        d_state = [lax.dot_general(kt_s[rows, hs(h)], v_new[h], tn_dims, preferred_element_type=F32)
                   for h in heads]
        for h in heads:
            state[h] = s[h] * e_last[:, h:h + 1] + d_state[h]
        for h in heads:
            o_ref[rows, hs(h)] = (_rms(o[h], norm_w) * _silu(z_ref[rows, hs(h)])).astype(o_ref.dtype)
        return carry

    lax.fori_loop(0, tm // c, advance, 0)


def gdn(proj, proj_small, q_col, ba_col, conv_w, a_log, dt_bias, norm_w, layer, *, tm, chunks_per_step,
        exact_inverse):
    s = proj.shape[0]
    width = GDN_HEADS * GDN_DV
    per_layer = lambda a: _per_layer(a, layer)
    col = lambda c: pl.BlockSpec((tm, width), lambda i: (i, c))
    return pl.pallas_call(
        functools.partial(_gdn_kernel, tm=tm, chunks_per_step=chunks_per_step,
                          exact_inverse=exact_inverse),
        grid=(s // tm,),
        in_specs=[
            col(q_col), col(q_col + 1), col(q_col + 2), col(q_col + 3),
            pl.BlockSpec((tm, LANES), lambda i: (i, ba_col)),
            per_layer(conv_w), per_layer(a_log), per_layer(dt_bias), per_layer(norm_w),
        ],
        out_specs=pl.BlockSpec((tm, width), lambda i: (i, 0)),
        out_shape=jax.ShapeDtypeStruct((s, width), BF16),
        scratch_shapes=[
            pltpu.VMEM((tm + SUBLANES, GDN_QKV_COLS), F32),
            pltpu.VMEM((tm, width), F32),
            pltpu.VMEM((tm, width), F32),
            pltpu.VMEM((tm, width), F32),
            pltpu.VMEM((tm, GDN_HEADS), F32),
            pltpu.VMEM((tm, GDN_HEADS), F32),
            pltpu.VMEM((tm, width), F32),
            pltpu.VMEM((2 * tm, width), BF16),
            pltpu.VMEM((tm, width), BF16),
            pltpu.VMEM((tm, GDN_HEADS * GDN_CHUNK), BF16),
            pltpu.VMEM((tm // GDN_CHUNK, GDN_HEADS), F32),
            pltpu.VMEM((GDN_HEADS, GDN_DK, GDN_DV), F32),
        ],
        compiler_params=_params("arbitrary"),
        name="gdn",
    )(proj, proj, proj, proj, proj_small, conv_w, a_log, dt_bias, norm_w)


def _merge_kernel(h_ref, ym_ref, yl_ref, yg_ref, wg0_ref, wg1_ref, wg2_ref, bg_ref, wb_ref, o_ref):
    bg = bg_ref[...]
    h = h_ref[...]
    acc = None
    for n, (y_ref, wg_ref) in enumerate(((ym_ref, wg0_ref), (yl_ref, wg1_ref), (yg_ref, wg2_ref))):
        gate = jax.nn.sigmoid(_dot(h, wg_ref[...]) + bg[n:n + 1, :])
        term = gate * _dot(y_ref[...], wb_ref[n])
        acc = term if acc is None else acc + term
    o_ref[...] = acc.astype(o_ref.dtype)


def merge(h, y_mla, y_lru, y_gdn, w_gate, b_gate, w_branch, layer, *, tm, tn):
    s, bw = y_mla.shape
    k = h.shape[1]
    d = w_branch.shape[3]
    nj = d // tn
    y_spec = pl.BlockSpec((tm, bw), lambda i, j: (i, 0))
    gate_spec = lambda n: pl.BlockSpec((None, k, tn), lambda i, j: (layer, 0, n * nj + j))
    return pl.pallas_call(
        _merge_kernel,
        grid=(s // tm, nj),
        in_specs=[pl.BlockSpec((tm, k), lambda i, j: (i, 0)), y_spec, y_spec, y_spec,
                  gate_spec(0), gate_spec(1), gate_spec(2),
                  pl.BlockSpec((None, N_BRANCH, tn), lambda i, j: (layer, 0, j)),
                  pl.BlockSpec((None, N_BRANCH, bw, tn), lambda i, j: (layer, 0, 0, j))],
        out_specs=pl.BlockSpec((tm, tn), lambda i, j: (i, j)),
        out_shape=jax.ShapeDtypeStruct((s, d), BF16),
        compiler_params=_params("arbitrary", "arbitrary"),
        name="merge",
    )(h, y_mla, y_lru, y_gdn, w_gate, w_gate, w_gate, b_gate, w_branch)


def _residual_matmul_kernel(x_ref, a_ref, w_ref, o_ref):
    o_ref[...] = x_ref[...] + _dot(a_ref[...], w_ref[...])


def residual_matmul(x, a, w, layer, *, tm, tn):
    s, k = a.shape
    n = w.shape[2]
    return pl.pallas_call(
        _residual_matmul_kernel,
        grid=(s // tm, n // tn),
        in_specs=[
            pl.BlockSpec((tm, tn), lambda i, j: (i, j)),
            pl.BlockSpec((tm, k), lambda i, j: (i, 0)),
            pl.BlockSpec((None, k, tn), lambda i, j: (layer, 0, j)),
        ],
        out_specs=pl.BlockSpec((tm, tn), lambda i, j: (i, j)),
        out_shape=jax.ShapeDtypeStruct((s, n), F32),
        compiler_params=_params("arbitrary", "arbitrary"),
        name="residual_matmul",
    )(x, a, w)


def _ffn_kernel(x_ref, g_ref, wg_ref, wu_ref, wd_ref, fg_ref, o_ref, h_ref, *, final_norm):
    j = pl.program_id(1)

    @pl.when(j == 0)
    def _():
        x = x_ref[...]
        h_ref[...] = _rms(x, g_ref[...]).astype(BF16)
        o_ref[...] = x

    h = h_ref[...]
    a = _silu(_dot(h, wg_ref[...])) * _dot(h, wu_ref[...])
    o_ref[...] += _dot(a.astype(BF16), wd_ref[...])

    if final_norm:
        @pl.when(j == pl.num_programs(1) - 1)
        def _():
            o_ref[...] = _rms(o_ref[...], fg_ref[...])


def ffn(x, gain, w_gate, w_up, w_down, final_gain, layer, *, tm, th, final_norm):
    s, d = x.shape
    hidden = w_gate.shape[2]
    return pl.pallas_call(
        functools.partial(_ffn_kernel, final_norm=final_norm),
        grid=(s // tm, hidden // th),
        in_specs=[
            pl.BlockSpec((tm, d), lambda i, j: (i, 0)),
            _per_layer(gain, layer),
            pl.BlockSpec((None, d, th), lambda i, j: (layer, 0, j)),
            pl.BlockSpec((None, d, th), lambda i, j: (layer, 0, j)),
            pl.BlockSpec((None, th, d), lambda i, j: (layer, j, 0)),
            pl.BlockSpec((1, d), lambda i, j: (0, 0)),
        ],
        out_specs=pl.BlockSpec((tm, d), lambda i, j: (i, 0)),
        out_shape=jax.ShapeDtypeStruct((s, d), F32),
        scratch_shapes=[pltpu.VMEM((tm, d), BF16)],
        compiler_params=_params("arbitrary", "arbitrary"),
        name="ffn",
    )(x, gain, w_gate, w_up, w_down, final_gain)


W_IN_MAIN_OFF = SMALL_B_OFF
W_IN_MAIN_COLS = 2 * LRU_WIDTH + 2 * GDN_HEADS * GDN_DK + 2 * GDN_HEADS * GDN_DV
W_IN_BA_OFF = W_IN_MAIN_OFF + W_IN_MAIN_COLS
W_IN_GATE_OFF = W_IN_BA_OFF + 2 * GDN_HEADS
W_IN_COLS = W_IN_GATE_OFF + GATE_COLS


def _regroup_w_in_kernel(main_ref, gate_ref, head_ref, ba_ref, small_o, main_o, gate_o):
    main_o[...] = main_ref[...].T.astype(BF16)
    gate_o[...] = gate_ref[...].T.astype(BF16)

    @pl.when(pl.program_id(1) == 0)
    def _():
        head = head_ref[...].T
        ba = ba_ref[...].T
        aligned = W_IN_MAIN_OFF // LANES * LANES
        pad = jnp.zeros((head.shape[0], SMALL_COLS - SMALL_A_OFF - GDN_HEADS), F32)
        tail = jnp.concatenate([head[:, aligned:W_IN_MAIN_OFF], ba[:, :2 * GDN_HEADS], pad], axis=1)
        small_o[:, :aligned] = head[:, :aligned].astype(BF16)
        small_o[:, aligned:] = tail.astype(BF16)


def regroup_w_in(w_in_t, *, tn):
    depth, n, d = w_in_t.shape
    assert n == W_IN_COLS and W_IN_MAIN_COLS == GATE_COLS
    window = lambda rows, start: pl.BlockSpec(
        (None, pl.Element(rows), pl.Element(d)), lambda l, i: (l, pl.multiple_of(start(i), SUBLANES), 0))
    out = lambda cols: pl.BlockSpec((None, d, cols), lambda l, i: (l, 0, i))
    return pl.pallas_call(
        _regroup_w_in_kernel,
        grid=(depth, GATE_COLS // tn),
        in_specs=[window(tn, lambda i: W_IN_MAIN_OFF + i * tn),
                  window(tn, lambda i: W_IN_GATE_OFF + i * tn),
                  window(SMALL_COLS, lambda i: 0),
                  window(LANES, lambda i: W_IN_BA_OFF)],
        out_specs=[pl.BlockSpec((None, d, SMALL_COLS), lambda l, i: (l, 0, 0)), out(tn), out(tn)],
        out_shape=[jax.ShapeDtypeStruct((depth, d, SMALL_COLS), BF16),
                   jax.ShapeDtypeStruct((depth, d, W_IN_MAIN_COLS), BF16),
                   jax.ShapeDtypeStruct((depth, d, GATE_COLS), BF16)],
        compiler_params=_params("arbitrary", "arbitrary"),
        name="regroup_w_in",
    )(w_in_t, w_in_t, w_in_t, w_in_t)


def _regroup_weights(w_in, w_uq, w_ukv, lru_w_a, lru_w_x):
    depth = w_in.shape[0]
    w_small, w_main, w_gate = regroup_w_in(jnp.swapaxes(w_in, 1, 2), tn=256)

    uq = w_uq.reshape(depth, Q_LORA, MLA_HEADS, QK_DIM)
    flat = lambda t: t.reshape(depth, t.shape[1], -1).astype(BF16)
    wqn = flat(uq[..., :QK_NOPE])
    wq1 = flat(uq[..., QK_NOPE:QK_NOPE + ROPE_HALF])
    wq2 = flat(uq[..., QK_NOPE + ROPE_HALF:])
    ukv = w_ukv.reshape(depth, KV_LORA, MLA_HEADS, QK_NOPE + V_HEAD)
    wk = flat(ukv[..., :QK_NOPE])
    wv = flat(ukv[..., QK_NOPE:])

    def pair_tiles(w):
        pairs = w.reshape(depth, LRU_BLOCKS // 2, 2, LRU_BLOCK_W, LRU_BLOCK_W)
        z = jnp.zeros_like(pairs[:, :, 0])
        top = jnp.concatenate([pairs[:, :, 0], z], axis=-1)
        bottom = jnp.concatenate([z, pairs[:, :, 1]], axis=-1)
        return jnp.concatenate([top, bottom], axis=-2)

    w_gates = jnp.concatenate([pair_tiles(lru_w_a), pair_tiles(lru_w_x)], axis=-1).astype(BF16)
    return w_main, w_gate, w_small, wqn, wq1, wq2, wk, wv, w_gates


def _rows(v):
    return v.reshape(v.shape[0], 1, -1)


def kernel(x, positions, norm_mix, w_in, mla_q_norm, mla_w_uq, mla_kv_norm, mla_w_ukv, lru_conv_w, lru_conv_b, lru_w_a, lru_b_a, lru_w_x, lru_b_x, lru_lambda, gdn_conv_w, gdn_a_log, gdn_dt_bias, gdn_norm, w_branch, b_gate, w_out, norm_ffn, ffn_w_gate, ffn_w_up, ffn_w_down, norm_final):
    b, s, d = x.shape
    assert b == 1
    depth = w_in.shape[0]
    xs = x.reshape(s, d)

    half = ROPE_HALF
    inv_freq = ROPE_THETA ** (-jnp.arange(half, dtype=F32) / half)
    inv_b = jnp.tile(inv_freq, MLA_HEADS).reshape(1, MLA_HEADS * half)
    pos_b = jnp.broadcast_to(positions.astype(F32).reshape(s, 1), (s, MLA_HEADS * half))
    cos, sin = rope_tables(pos_b, inv_b, tm=1024)

    lru_x_col, gdn_q_col = 0, 2
    ba_col = SMALL_ROPE_OFF // LANES

    w_main, w_gate, w_small, wqn, wq1, wq2, wk, wv, w_gates = _regroup_weights(
        w_in, mla_w_uq, mla_w_ukv, lru_w_a, lru_w_x)
    w_branch_b = w_branch.astype(BF16)
    w_out_b = w_out.astype(BF16)
    ffn_gate_b = ffn_w_gate.astype(BF16)
    ffn_up_b = ffn_w_up.astype(BF16)
    ffn_down_b = ffn_w_down.astype(BF16)
    gain_mix, gain_ffn = _rows(norm_mix), _rows(norm_ffn)
    gain_q, gain_kv = _rows(mla_q_norm), _rows(mla_kv_norm)
    lru_cb, lru_ba, lru_bx, lru_lam = _rows(lru_conv_b), _rows(lru_b_a), _rows(lru_b_x), _rows(lru_lambda)
    gdn_alog, gdn_dtb, gdn_nw = _rows(gdn_a_log), _rows(gdn_dt_bias), _rows(gdn_norm)
    gain_final = norm_final.reshape(1, d)

    for l in range(depth):
        proj_small, h = norm_matmul(xs, gain_mix, w_small, l, tm=1024, tn=SMALL_COLS)
        proj = matmul(h, w_main, l, tm=MAIN_TM, tn=MAIN_TN)

        q, k, v = mla_prep(proj_small, gain_q, gain_kv, wqn, wq1, wq2, wk, wv, cos, sin, l, tm=512)
        y_mla = attention(q, k, v, tr=1024, tk=1024, n_sub=2)
        y_lru = rglru(proj, lru_x_col, lru_x_col + 1, lru_conv_w, lru_cb, w_gates,
                      lru_ba, lru_bx, lru_lam, l, tm=512)
        y_gdn = gdn(proj, proj_small, gdn_q_col, ba_col, gdn_conv_w, gdn_alog, gdn_dtb, gdn_nw, l,
                    tm=512, chunks_per_step=2, exact_inverse=False)

        mixed = merge(h, y_mla, y_lru, y_gdn, w_gate, b_gate, w_branch_b, l, tm=MERGE_TM, tn=MERGE_TN)
        xs = residual_matmul(xs, mixed, w_out_b, l, tm=OUT_TM, tn=OUT_TN)
        xs = ffn(xs, gain_ffn, ffn_gate_b, ffn_up_b, ffn_down_b, gain_final, l, tm=FFN_TM, th=FFN_TH,
                 final_norm=(l == depth - 1))
    return xs.reshape(b, s, d)
```

```python
import functools
import math

import jax
import jax.numpy as jnp
from jax import lax
from jax.experimental import pallas as pl
from jax.experimental.pallas import tpu as pltpu

F32 = jnp.float32
BF16 = jnp.bfloat16
HIGHEST = lax.Precision.HIGHEST

EPS = 1e-6
N_BRANCH = 3
MLA_HEADS = 8
QK_NOPE = 128
QK_ROPE = 64
ROPE_HALF = QK_ROPE // 2
QK_DIM = QK_NOPE + QK_ROPE
V_HEAD = 128
Q_LORA = 512
KV_LORA = 512
ROPE_THETA = 10000.0
LRU_WIDTH = 1024
LRU_BLOCKS = 8
LRU_BLOCK_W = LRU_WIDTH // LRU_BLOCKS
LRU_C = 8.0
CONV_W = 4
GDN_HEADS = 8
GDN_DK = 128
GDN_DV = 128
GDN_CHUNK = 64
GDN_INV_BLOCK = 16

V7X_VMEM_BYTES = 64 * 1024 * 1024
VMEM_LIMIT_BYTES = V7X_VMEM_BYTES - 8 * 1024 * 1024
SUBLANES = 8
LANES = 128

MAIN_TM, MAIN_TN = 2048, 1024
MERGE_TM, MERGE_TN = 1024, 512
OUT_TM, OUT_TN = 1024, 1024
FFN_TM, FFN_TH = 512, 512

GATE_COLS = N_BRANCH * 2048
GDN_QKV_COLS = 3 * GDN_HEADS * GDN_DK
SMALL_COLS = 1152
SMALL_ROPE_OFF = Q_LORA + KV_LORA
SMALL_B_OFF = SMALL_ROPE_OFF + QK_ROPE
SMALL_A_OFF = SMALL_B_OFF + GDN_HEADS


def _params(*semantics):
    return pltpu.CompilerParams(dimension_semantics=semantics, vmem_limit_bytes=VMEM_LIMIT_BYTES)


def _dot(a, b, precision=None):
    return jnp.dot(a, b, preferred_element_type=F32, precision=precision)


def _rms(x, gain):
    return x * lax.rsqrt(jnp.mean(x * x, axis=-1, keepdims=True) + EPS) * gain


def _softplus(x):
    return jnp.maximum(x, 0.0) + jnp.log1p(jnp.exp(-jnp.abs(x)))


def _silu(x):
    return x * jax.nn.sigmoid(x)


def _gelu_tanh(x):
    return 0.5 * x * (1.0 + jnp.tanh(math.sqrt(2.0 / math.pi) * (x + 0.044715 * (x * x * x))))


def _norm_matmul_kernel(x_ref, g_ref, w_ref, o_ref, h_ref):
    @pl.when(pl.program_id(1) == 0)
    def _():
        h_ref[...] = _rms(x_ref[...], g_ref[...]).astype(BF16)

    o_ref[...] = _dot(h_ref[...], w_ref[...]).astype(o_ref.dtype)


def _per_layer(a, layer):
    return pl.BlockSpec((None,) + a.shape[1:], lambda *_: (layer,) + (0,) * (a.ndim - 1))


def norm_matmul(x, gain, w, layer, *, tm, tn):
    s, k = x.shape
    n = w.shape[2]
    return pl.pallas_call(
        _norm_matmul_kernel,
        grid=(s // tm, n // tn),
        in_specs=[
            pl.BlockSpec((tm, k), lambda i, j: (i, 0)),
            _per_layer(gain, layer),
            pl.BlockSpec((None, k, tn), lambda i, j: (layer, 0, j)),
        ],
        out_specs=[pl.BlockSpec((tm, tn), lambda i, j: (i, j)),
                   pl.BlockSpec((tm, k), lambda i, j: (i, 0))],
        out_shape=[jax.ShapeDtypeStruct((s, n), F32), jax.ShapeDtypeStruct((s, k), BF16)],
        compiler_params=_params("arbitrary", "arbitrary"),
        name="norm_matmul",
    )(x, gain, w)


def _matmul_kernel(a_ref, w_ref, o_ref):
    o_ref[...] = _dot(a_ref[...], w_ref[...])


def matmul(a, w, layer, *, tm, tn):
    s, k = a.shape
    n = w.shape[2]
    return pl.pallas_call(
        _matmul_kernel,
        grid=(s // tm, n // tn),
        in_specs=[
            pl.BlockSpec((tm, k), lambda i, j: (i, 0)),
            pl.BlockSpec((None, k, tn), lambda i, j: (layer, 0, j)),
        ],
        out_specs=pl.BlockSpec((tm, tn), lambda i, j: (i, j)),
        out_shape=jax.ShapeDtypeStruct((s, n), F32),
        compiler_params=_params("arbitrary", "arbitrary"),
        name="matmul",
    )(a, w)


def _rope_table_kernel(pos_ref, inv_ref, cos_ref, sin_ref):
    ang = pos_ref[...] * inv_ref[...]
    cos_ref[...] = jnp.cos(ang)
    sin_ref[...] = jnp.sin(ang)


def rope_tables(pos_b, inv_b, *, tm):
    s, n = pos_b.shape
    blk = pl.BlockSpec((tm, n), lambda i: (i, 0))
    return pl.pallas_call(
        _rope_table_kernel,
        grid=(s // tm,),
        in_specs=[blk, pl.BlockSpec((1, n), lambda i: (0, 0))],
        out_specs=[blk, blk],
        out_shape=[jax.ShapeDtypeStruct((s, n), F32)] * 2,
        compiler_params=_params("arbitrary"),
        name="rope_tables",
    )(pos_b, inv_b)


def _mla_prep_kernel(p_ref, qg_ref, kvg_ref, wqn_ref, wq1_ref, wq2_ref, wk_ref, wv_ref,
                     cos_ref, sin_ref, q_out, k_out, v_out):
    p = p_ref[...]
    hq = _rms(p[:, :Q_LORA], qg_ref[...]).astype(BF16)
    hkv = _rms(p[:, Q_LORA:SMALL_ROPE_OFF], kvg_ref[...]).astype(BF16)
    kr = p[:, SMALL_ROPE_OFF:SMALL_ROPE_OFF + QK_ROPE]
    cos = cos_ref[...]
    sin = sin_ref[...]
    qn = _dot(hq, wqn_ref[...])
    q1 = _dot(hq, wq1_ref[...])
    q2 = _dot(hq, wq2_ref[...])
    qr1 = q1 * cos - q2 * sin
    qr2 = q2 * cos + q1 * sin
    kn = _dot(hkv, wk_ref[...])
    v = _dot(hkv, wv_ref[...])
    c32 = cos[:, :ROPE_HALF]
    s32 = sin[:, :ROPE_HALF]
    k1 = kr[:, :ROPE_HALF]
    k2 = kr[:, ROPE_HALF:]
    kr1 = k1 * c32 - k2 * s32
    kr2 = k2 * c32 + k1 * s32
    for h in range(MLA_HEADS):
        nope = slice(h * QK_NOPE, (h + 1) * QK_NOPE)
        rope = slice(h * ROPE_HALF, (h + 1) * ROPE_HALF)
        q_out[h] = jnp.concatenate([qn[:, nope], qr1[:, rope], qr2[:, rope]], axis=-1).astype(BF16)
        k_out[h] = jnp.concatenate([kn[:, nope], kr1, kr2], axis=-1).astype(BF16)
        v_out[h] = v[:, h * V_HEAD:(h + 1) * V_HEAD].astype(BF16)


def mla_prep(proj_small, q_gain, kv_gain, wqn, wq1, wq2, wk, wv, cos, sin, layer, *, tm):
    s = proj_small.shape[0]
    full = lambda a: _per_layer(a, layer)
    rows = lambda n: pl.BlockSpec((tm, n), lambda i: (i, 0))
    heads = lambda d: pl.BlockSpec((MLA_HEADS, tm, d), lambda i: (0, i, 0))
    return pl.pallas_call(
        _mla_prep_kernel,
        grid=(s // tm,),
        in_specs=[rows(SMALL_COLS), full(q_gain), full(kv_gain), full(wqn), full(wq1), full(wq2),
                  full(wk), full(wv), rows(cos.shape[1]), rows(sin.shape[1])],
        out_specs=[heads(QK_DIM), heads(QK_DIM), heads(V_HEAD)],
        out_shape=[jax.ShapeDtypeStruct((MLA_HEADS, s, QK_DIM), BF16),
                   jax.ShapeDtypeStruct((MLA_HEADS, s, QK_DIM), BF16),
                   jax.ShapeDtypeStruct((MLA_HEADS, s, V_HEAD), BF16)],
        compiler_params=_params("arbitrary"),
        name="mla_prep",
    )(proj_small, q_gain, kv_gain, wqn, wq1, wq2, wk, wv, cos, sin)


def _attention_kernel(q_ref, k_ref, v_ref, o_ref, *, tr, tk, n_sub, exp2_scale):
    i = pl.program_id(1)
    tq = tr * n_sub
    qs = [q_ref[0, a * tr:(a + 1) * tr, :] for a in range(n_sub)]

    def kv_block(j):
        start = pl.multiple_of(j * tk, tk)
        return k_ref[0, pl.ds(start, tk), :], v_ref[0, pl.ds(start, tk), :]

    def update(q, kv, carry, mask_offset):
        m, l, acc = carry
        kj, vj = kv
        s = lax.dot_general(q, kj, (((1,), (1,)), ((), ())), preferred_element_type=F32)
        if mask_offset is not None:
            row = lax.broadcasted_iota(jnp.int32, (tr, tk), 0)
            col = lax.broadcasted_iota(jnp.int32, (tr, tk), 1)
            s = jnp.where(col + mask_offset <= row, s, -jnp.inf)
        m_new = jnp.maximum(m, jnp.max(s, axis=-1, keepdims=True))
        p = jnp.exp2((s - m_new) * exp2_scale)
        alpha = jnp.exp2((m - m_new) * exp2_scale)
        l = alpha * l + jnp.sum(p, axis=-1, keepdims=True)
        acc = alpha * acc + _dot(p.astype(BF16), vj)
        return m_new, l, acc

    def full_blocks(j, carries):
        kv = kv_block(j)
        return tuple(update(qs[a], kv, carries[a], None) for a in range(n_sub))

    init = (jnp.full((tr, 1), -jnp.inf, F32), jnp.zeros((tr, 1), F32), jnp.zeros((tr, V_HEAD), F32))
    n_past = i * (tq // tk)
    carries = list(lax.fori_loop(0, n_past, full_blocks, (init,) * n_sub))
    for b in range(tq // tk):
        kv = kv_block(n_past + b)
        c0 = b * tk
        for a in range(n_sub):
            r0 = a * tr
            if c0 > r0 + tr - 1:
                continue
            whole = c0 + tk - 1 <= r0
            carries[a] = update(qs[a], kv, carries[a], None if whole else c0 - r0)
    for a in range(n_sub):
        _, l, acc = carries[a]
        o_ref[a * tr:(a + 1) * tr, :] = (acc / l).astype(o_ref.dtype)


def attention(q, k, v, *, tr, tk, n_sub):
    h, s, _ = q.shape
    tq = tr * n_sub
    assert tq % tk == 0
    exp2_scale = QK_DIM ** -0.5 * math.log2(math.e)
    return pl.pallas_call(
        functools.partial(_attention_kernel, tr=tr, tk=tk, n_sub=n_sub, exp2_scale=exp2_scale),
        grid=(h, s // tq),
        in_specs=[
            pl.BlockSpec((1, tq, QK_DIM), lambda hh, i: (hh, i, 0)),
            pl.BlockSpec((1, s, QK_DIM), lambda hh, i: (hh, 0, 0)),
            pl.BlockSpec((1, s, V_HEAD), lambda hh, i: (hh, 0, 0)),
        ],
        out_specs=pl.BlockSpec((tq, V_HEAD), lambda hh, i: (i, hh)),
        out_shape=jax.ShapeDtypeStruct((s, h * V_HEAD), BF16),
        compiler_params=_params("arbitrary", "arbitrary"),
        name="attention",
    )(q, k, v)


def _causal_conv_tile(x_refs, cw_ref, xbuf, tm):
    @pl.when(pl.program_id(0) == 0)
    def _():
        xbuf[0:SUBLANES, :] = jnp.zeros((SUBLANES, xbuf.shape[1]), F32)

    off = 0
    for x_ref in x_refs:
        xbuf[SUBLANES:SUBLANES + tm, off:off + x_ref.shape[1]] = x_ref[...]
        off += x_ref.shape[1]
    cw = cw_ref[...]
    acc = cw[0:1, :] * xbuf[pl.ds(SUBLANES - CONV_W + 1, tm), :]
    for k in range(1, CONV_W):
        acc = acc + cw[k:k + 1, :] * xbuf[pl.ds(SUBLANES - CONV_W + 1 + k, tm), :]
    xbuf[0:SUBLANES, :] = xbuf[tm:tm + SUBLANES, :]
    return acc


def _rglru_kernel(x_ref, y_ref, cw_ref, cb_ref, wg_ref, ba_ref, bx_ref, lam_ref, o_ref,
                  xbuf, a_s, b_s, h_carry, *, tm):
    @pl.when(pl.program_id(0) == 0)
    def _():
        h_carry[...] = jnp.zeros(h_carry.shape, F32)

    xc = _causal_conv_tile([x_ref], cw_ref, xbuf, tm) + cb_ref[...]
    xcb = xc.astype(BF16)
    pair_w = 2 * LRU_BLOCK_W
    r_parts, i_parts = [], []
    for p in range(LRU_BLOCKS // 2):
        g = _dot(xcb[:, p * pair_w:(p + 1) * pair_w], wg_ref[p])
        r_parts.append(g[:, :pair_w])
        i_parts.append(g[:, pair_w:])
    r = jax.nn.sigmoid(jnp.concatenate(r_parts, axis=-1) + ba_ref[...])
    gi = jax.nn.sigmoid(jnp.concatenate(i_parts, axis=-1) + bx_ref[...])
    log_a = -LRU_C * r * _softplus(-lam_ref[...])
    t = jnp.tanh(log_a)
    mult = jnp.sqrt(-2.0 * t / (1.0 - t))
    a_s[...] = jnp.exp(log_a)
    b_s[...] = mult * (gi * xc)

    row = lax.broadcasted_iota(jnp.int32, (SUBLANES, LRU_WIDTH), 0)

    def group(gidx, h_prev):
        r0 = pl.multiple_of(gidx * SUBLANES, SUBLANES)
        a = a_s[pl.ds(r0, SUBLANES), :]
        b = b_s[pl.ds(r0, SUBLANES), :]
        shift = 1
        while shift < SUBLANES:
            keep = row >= shift
            a_sh = jnp.where(keep, pltpu.roll(a, shift, 0), 1.0)
            b_sh = jnp.where(keep, pltpu.roll(b, shift, 0), 0.0)
            b = a * b_sh + b
            a = a * a_sh
            shift *= 2
        hs = a * h_prev + b
        b_s[pl.ds(r0, SUBLANES), :] = hs
        return hs[SUBLANES - 1:SUBLANES, :]

    h_carry[...] = lax.fori_loop(0, tm // SUBLANES, group, h_carry[...])
    o_ref[...] = (b_s[...] * _gelu_tanh(y_ref[...])).astype(o_ref.dtype)


def rglru(proj, x_col, y_col, conv_w, conv_b, w_gates, b_a, b_x, lam, layer, *, tm):
    s = proj.shape[0]
    full = lambda a: _per_layer(a, layer)
    return pl.pallas_call(
        functools.partial(_rglru_kernel, tm=tm),
        grid=(s // tm,),
        in_specs=[
            pl.BlockSpec((tm, LRU_WIDTH), lambda i: (i, x_col)),
            pl.BlockSpec((tm, LRU_WIDTH), lambda i: (i, y_col)),
            full(conv_w), full(conv_b), full(w_gates), full(b_a), full(b_x), full(lam),
        ],
        out_specs=pl.BlockSpec((tm, LRU_WIDTH), lambda i: (i, 0)),
        out_shape=jax.ShapeDtypeStruct((s, LRU_WIDTH), BF16),
        scratch_shapes=[
            pltpu.VMEM((tm + SUBLANES, LRU_WIDTH), F32),
            pltpu.VMEM((tm, LRU_WIDTH), F32),
            pltpu.VMEM((tm, LRU_WIDTH), F32),
            pltpu.VMEM((1, LRU_WIDTH), F32),
        ],
        compiler_params=_params("arbitrary"),
        name="rglru",
    )(proj, proj, conv_w, conv_b, w_gates, b_a, b_x, lam)


def _unit_lower_solve(l_mats, rhs, block_diag, cast, mm):
    heads = range(len(l_mats))
    d = [jnp.where(block_diag, l_mats[h], 0.0) for h in heads]
    n = [l_mats[h] - d[h] for h in heads]
    dc = [cast(d[h]) for h in heads]
    d2 = [mm(dc[h], dc[h]) for h in heads]
    d2c = [cast(d2[h]) for h in heads]
    d4 = [mm(d2c[h], d2c[h]) for h in heads]
    d4c = [cast(d4[h]) for h in heads]
    d8 = [mm(d4c[h], d4c[h]) for h in heads]
    d8c = [cast(d8[h]) for h in heads]
    x = [-d[h] for h in heads]
    for dp, dpc in ((d2, d2c), (d4, d4c), (d8, d8c)):
        x = [x[h] + dp[h] + mm(cast(x[h]), dpc[h]) for h in heads]
    xc = [cast(x[h]) for h in heads]
    m = [n[h] + mm(xc[h], cast(n[h])) for h in heads]
    mc = [cast(m[h]) for h in heads]
    m2 = [mm(mc[h], mc[h]) for h in heads]
    y = [m2[h] - m[h] - mm(mc[h], cast(m2[h])) for h in heads]
    r1 = [rhs[h] + mm(xc[h], cast(rhs[h])) for h in heads]
    return [r1[h] + mm(cast(y[h]), cast(r1[h])) for h in heads]


def _gdn_kernel(qin_ref, kin_ref, vin_ref, z_ref, ba_ref, cw_ref, alog_ref, dtb_ref, nw_ref, o_ref,
                xbuf, q_s, k_s, v_s, g_s, beta_s, u_s, wq_s, kt_s, qk_s, el_s, state,
                *, tm, chunks_per_step, exact_inverse):
    @pl.when(pl.program_id(0) == 0)
    def _():
        state[...] = jnp.zeros(state.shape, F32)

    act = _silu(_causal_conv_tile([qin_ref, kin_ref, vin_ref], cw_ref, xbuf, tm))
    kw = GDN_HEADS * GDN_DK
    for h in range(GDN_HEADS):
        sl = slice(h * GDN_DK, (h + 1) * GDN_DK)
        qh = act[:, h * GDN_DK:(h + 1) * GDN_DK]
        kh = act[:, kw + h * GDN_DK:kw + (h + 1) * GDN_DK]
        q_s[:, sl] = qh * lax.rsqrt(jnp.sum(qh * qh, axis=-1, keepdims=True) + EPS) * (GDN_DK ** -0.5)
        k_s[:, sl] = kh * lax.rsqrt(jnp.sum(kh * kh, axis=-1, keepdims=True) + EPS)
    v_s[...] = act[:, 2 * kw:]
    ba = ba_ref[...]
    beta_s[...] = jax.nn.sigmoid(ba[:, SMALL_B_OFF % LANES:SMALL_B_OFF % LANES + GDN_HEADS])
    a_logit = ba[:, SMALL_A_OFF % LANES:SMALL_A_OFF % LANES + GDN_HEADS]
    g_s[...] = -jnp.exp(alog_ref[...]) * _softplus(a_logit + dtb_ref[...])

    c = GDN_CHUNK
    row = lax.broadcasted_iota(jnp.int32, (c, c), 0)
    col = lax.broadcasted_iota(jnp.int32, (c, c), 1)
    incl = row >= col
    strict = row > col
    tril = incl.astype(F32)
    stril = strict.astype(F32)
    block_diag = jnp.logical_and(strict, (row // GDN_INV_BLOCK) == (col // GDN_INV_BLOCK))
    norm_w = nw_ref[...]
    nt_dims = (((1,), (1,)), ((), ()))
    tn_dims = (((0,), (0,)), ((), ()))
    heads = range(GDN_HEADS)
    hs = lambda h: slice(h * GDN_DK, (h + 1) * GDN_DK)
    cs = lambda h: slice(h * c, (h + 1) * c)
    bf = lambda t: t.astype(BF16)
    if exact_inverse:
        cast = lambda t: t
        mm = lambda a, b: _dot(a, b, HIGHEST)
    else:
        cast = bf
        mm = _dot

    def chunk_operands(ci):
        r0 = pl.multiple_of(ci * c, c)
        rows = pl.ds(r0, c)
        g_c = g_s[rows, :]
        beta_c = beta_s[rows, :]
        gc = _dot(tril, g_c, HIGHEST)
        gb = jnp.concatenate([jnp.broadcast_to(g_c[:, h:h + 1], (c, c)) * stril for h in heads], axis=1)
        diff = _dot(tril, gb, HIGHEST)
        e_gc = jnp.exp(gc)
        g_last = gc[c - 1:c, :]
        e_tail = jnp.exp(g_last - gc)
        el_s[pl.ds(ci, 1), :] = jnp.exp(g_last)
        q = [q_s[rows, hs(h)] for h in heads]
        k = [k_s[rows, hs(h)] for h in heads]
        bh = [jnp.broadcast_to(beta_c[:, h:h + 1], (c, GDN_DK)) for h in heads]
        eg = [jnp.broadcast_to(e_gc[:, h:h + 1], (c, GDN_DK)) for h in heads]
        kb = [k[h] * bh[h] for h in heads]
        decay = [jnp.where(incl, jnp.exp(diff[:, cs(h)]), 0.0) for h in heads]
        kq = [lax.dot_general(bf(jnp.concatenate([kb[h], q[h]], axis=0)), bf(k[h]), nt_dims,
                              preferred_element_type=F32) for h in heads]
        l_mats = [jnp.where(strict, kq[h][:c] * decay[h], 0.0) for h in heads]
        for h in heads:
            qk_s[rows, cs(h)] = bf(kq[h][c:] * decay[h])
            kt_s[rows, hs(h)] = bf(k[h] * jnp.broadcast_to(e_tail[:, h:h + 1], (c, GDN_DK)))
        rhs = [jnp.concatenate([v_s[rows, hs(h)] * bh[h], kb[h] * eg[h]], axis=1) for h in heads]
        q_dec = [q[h] * eg[h] for h in heads]
        return r0, l_mats, rhs, q_dec

    def prepare(step, carry):
        parts = [chunk_operands(step * chunks_per_step + t) for t in range(chunks_per_step)]
        l_mats = [l for part in parts for l in part[1]]
        rhs = [r for part in parts for r in part[2]]
        uw = _unit_lower_solve(l_mats, rhs, block_diag, cast, mm)
        for t, (r0, _, _, q_dec) in enumerate(parts):
            rows = pl.ds(r0, c)
            rows_w = pl.ds(pl.multiple_of(2 * r0, 2 * c), c)
            rows_q = pl.ds(pl.multiple_of(2 * r0 + c, c), c)
            for h in heads:
                uw_h = uw[t * GDN_HEADS + h]
                u_s[rows, hs(h)] = uw_h[:, :GDN_DV]
                wq_s[rows_w, hs(h)] = bf(uw_h[:, GDN_DV:])
                wq_s[rows_q, hs(h)] = bf(q_dec[h])
        return carry

    lax.fori_loop(0, tm // (c * chunks_per_step), prepare, 0)

    def advance(ci, carry):
        r0 = pl.multiple_of(ci * c, c)
        rows = pl.ds(r0, c)
        rows_wq = pl.ds(pl.multiple_of(2 * r0, 2 * c), 2 * c)
        e_last = el_s[pl.ds(ci, 1), :]
        s = [state[h] for h in heads]
        ws_qs = [_dot(wq_s[rows_wq, hs(h)], bf(s[h])) for h in heads]
        v_new = [bf(u_s[rows, hs(h)] - ws_qs[h][:c]) for h in heads]
        o = [ws_qs[h][c:] + _dot(qk_s[rows, cs(h)], v_new[h]) for h in heads]
        d_state = [lax.dot_general(kt_s[rows, hs(h)], v_new[h], tn_dims, preferred_element_type=F32)
                   for h in heads]
        for h in heads:
            state[h] = s[h] * e_last[:, h:h + 1] + d_state[h]
        for h in heads:
            o_ref[rows, hs(h)] = (_rms(o[h], norm_w) * _silu(z_ref[rows, hs(h)])).astype(o_ref.dtype)
        return carry

    lax.fori_loop(0, tm // c, advance, 0)


def gdn(proj, proj_small, q_col, ba_col, conv_w, a_log, dt_bias, norm_w, layer, *, tm, chunks_per_step,
        exact_inverse):
    s = proj.shape[0]
    width = GDN_HEADS * GDN_DV
    per_layer = lambda a: _per_layer(a, layer)
    col = lambda c: pl.BlockSpec((tm, width), lambda i: (i, c))
    return pl.pallas_call(
        functools.partial(_gdn_kernel, tm=tm, chunks_per_step=chunks_per_step,
                          exact_inverse=exact_inverse),
        grid=(s // tm,),
        in_specs=[
            col(q_col), col(q_col + 1), col(q_col + 2), col(q_col + 3),
            pl.BlockSpec((tm, LANES), lambda i: (i, ba_col)),
            per_layer(conv_w), per_layer(a_log), per_layer(dt_bias), per_layer(norm_w),
        ],
        out_specs=pl.BlockSpec((tm, width), lambda i: (i, 0)),
        out_shape=jax.ShapeDtypeStruct((s, width), BF16),
        scratch_shapes=[
            pltpu.VMEM((tm + SUBLANES, GDN_QKV_COLS), F32),
            pltpu.VMEM((tm, width), F32),
            pltpu.VMEM((tm, width), F32),
            pltpu.VMEM((tm, width), F32),
            pltpu.VMEM((tm, GDN_HEADS), F32),
            pltpu.VMEM((tm, GDN_HEADS), F32),
            pltpu.VMEM((tm, width), F32),
            pltpu.VMEM((2 * tm, width), BF16),
            pltpu.VMEM((tm, width), BF16),
            pltpu.VMEM((tm, GDN_HEADS * GDN_CHUNK), BF16),
            pltpu.VMEM((tm // GDN_CHUNK, GDN_HEADS), F32),
            pltpu.VMEM((GDN_HEADS, GDN_DK, GDN_DV), F32),
        ],
        compiler_params=_params("arbitrary"),
        name="gdn",
    )(proj, proj, proj, proj, proj_small, conv_w, a_log, dt_bias, norm_w)


def _merge_kernel(h_ref, ym_ref, yl_ref, yg_ref, wg0_ref, wg1_ref, wg2_ref, bg_ref, wb_ref, o_ref):
    bg = bg_ref[...]
    h = h_ref[...]
    acc = None
    for n, (y_ref, wg_ref) in enumerate(((ym_ref, wg0_ref), (yl_ref, wg1_ref), (yg_ref, wg2_ref))):
        gate = jax.nn.sigmoid(_dot(h, wg_ref[...]) + bg[n:n + 1, :])
        term = gate * _dot(y_ref[...], wb_ref[n])
        acc = term if acc is None else acc + term
    o_ref[...] = acc.astype(o_ref.dtype)


def merge(h, y_mla, y_lru, y_gdn, w_gate, b_gate, w_branch, layer, *, tm, tn):
    s, bw = y_mla.shape
    k = h.shape[1]
    d = w_branch.shape[3]
    nj = d // tn
    y_spec = pl.BlockSpec((tm, bw), lambda i, j: (i, 0))
    gate_spec = lambda n: pl.BlockSpec((None, k, tn), lambda i, j: (layer, 0, n * nj + j))
    return pl.pallas_call(
        _merge_kernel,
        grid=(s // tm, nj),
        in_specs=[pl.BlockSpec((tm, k), lambda i, j: (i, 0)), y_spec, y_spec, y_spec,
                  gate_spec(0), gate_spec(1), gate_spec(2),
                  pl.BlockSpec((None, N_BRANCH, tn), lambda i, j: (layer, 0, j)),
                  pl.BlockSpec((None, N_BRANCH, bw, tn), lambda i, j: (layer, 0, 0, j))],
        out_specs=pl.BlockSpec((tm, tn), lambda i, j: (i, j)),
        out_shape=jax.ShapeDtypeStruct((s, d), BF16),
        compiler_params=_params("arbitrary", "arbitrary"),
        name="merge",
    )(h, y_mla, y_lru, y_gdn, w_gate, w_gate, w_gate, b_gate, w_branch)


def _residual_matmul_kernel(x_ref, a_ref, w_ref, o_ref):
    o_ref[...] = x_ref[...] + _dot(a_ref[...], w_ref[...])


def residual_matmul(x, a, w, layer, *, tm, tn):
    s, k = a.shape
    n = w.shape[2]
    return pl.pallas_call(
        _residual_matmul_kernel,
        grid=(s // tm, n // tn),
        in_specs=[
            pl.BlockSpec((tm, tn), lambda i, j: (i, j)),
            pl.BlockSpec((tm, k), lambda i, j: (i, 0)),
            pl.BlockSpec((None, k, tn), lambda i, j: (layer, 0, j)),
        ],
        out_specs=pl.BlockSpec((tm, tn), lambda i, j: (i, j)),
        out_shape=jax.ShapeDtypeStruct((s, n), F32),
        compiler_params=_params("arbitrary", "arbitrary"),
        name="residual_matmul",
    )(x, a, w)


def _ffn_kernel(x_ref, g_ref, wg_ref, wu_ref, wd_ref, fg_ref, o_ref, h_ref, *, final_norm):
    j = pl.program_id(1)

    @pl.when(j == 0)
    def _():
        x = x_ref[...]
        h_ref[...] = _rms(x, g_ref[...]).astype(BF16)
        o_ref[...] = x

    h = h_ref[...]
    a = _silu(_dot(h, wg_ref[...])) * _dot(h, wu_ref[...])
    o_ref[...] += _dot(a.astype(BF16), wd_ref[...])

    if final_norm:
        @pl.when(j == pl.num_programs(1) - 1)
        def _():
            o_ref[...] = _rms(o_ref[...], fg_ref[...])


def ffn(x, gain, w_gate, w_up, w_down, final_gain, layer, *, tm, th, final_norm):
    s, d = x.shape
    hidden = w_gate.shape[2]
    return pl.pallas_call(
        functools.partial(_ffn_kernel, final_norm=final_norm),
        grid=(s // tm, hidden // th),
        in_specs=[
            pl.BlockSpec((tm, d), lambda i, j: (i, 0)),
            _per_layer(gain, layer),
            pl.BlockSpec((None, d, th), lambda i, j: (layer, 0, j)),
            pl.BlockSpec((None, d, th), lambda i, j: (layer, 0, j)),
            pl.BlockSpec((None, th, d), lambda i, j: (layer, j, 0)),
            pl.BlockSpec((1, d), lambda i, j: (0, 0)),
        ],
        out_specs=pl.BlockSpec((tm, d), lambda i, j: (i, 0)),
        out_shape=jax.ShapeDtypeStruct((s, d), F32),
        scratch_shapes=[pltpu.VMEM((tm, d), BF16)],
        compiler_params=_params("arbitrary", "arbitrary"),
        name="ffn",
    )(x, gain, w_gate, w_up, w_down, final_gain)


W_IN_MAIN_OFF = SMALL_B_OFF
W_IN_MAIN_COLS = 2 * LRU_WIDTH + 2 * GDN_HEADS * GDN_DK + 2 * GDN_HEADS * GDN_DV
W_IN_BA_OFF = W_IN_MAIN_OFF + W_IN_MAIN_COLS
W_IN_GATE_OFF = W_IN_BA_OFF + 2 * GDN_HEADS
W_IN_COLS = W_IN_GATE_OFF + GATE_COLS


def _regroup_w_in_kernel(main_ref, gate_ref, head_ref, ba_ref, small_o, main_o, gate_o):
    main_o[...] = main_ref[...].T.astype(BF16)
    gate_o[...] = gate_ref[...].T.astype(BF16)

    @pl.when(pl.program_id(1) == 0)
    def _():
        head = head_ref[...].T
        ba = ba_ref[...].T
        aligned = W_IN_MAIN_OFF // LANES * LANES
        pad = jnp.zeros((head.shape[0], SMALL_COLS - SMALL_A_OFF - GDN_HEADS), F32)
        tail = jnp.concatenate([head[:, aligned:W_IN_MAIN_OFF], ba[:, :2 * GDN_HEADS], pad], axis=1)
        small_o[:, :aligned] = head[:, :aligned].astype(BF16)
        small_o[:, aligned:] = tail.astype(BF16)


def regroup_w_in(w_in_t, *, tn):
    depth, n, d = w_in_t.shape
    assert n == W_IN_COLS and W_IN_MAIN_COLS == GATE_COLS
    window = lambda rows, start: pl.BlockSpec(
        (None, pl.Element(rows), pl.Element(d)), lambda l, i: (l, pl.multiple_of(start(i), SUBLANES), 0))
    out = lambda cols: pl.BlockSpec((None, d, cols), lambda l, i: (l, 0, i))
    return pl.pallas_call(
        _regroup_w_in_kernel,
        grid=(depth, GATE_COLS // tn),
        in_specs=[window(tn, lambda i: W_IN_MAIN_OFF + i * tn),
                  window(tn, lambda i: W_IN_GATE_OFF + i * tn),
                  window(SMALL_COLS, lambda i: 0),
                  window(LANES, lambda i: W_IN_BA_OFF)],
        out_specs=[pl.BlockSpec((None, d, SMALL_COLS), lambda l, i: (l, 0, 0)), out(tn), out(tn)],
        out_shape=[jax.ShapeDtypeStruct((depth, d, SMALL_COLS), BF16),
                   jax.ShapeDtypeStruct((depth, d, W_IN_MAIN_COLS), BF16),
                   jax.ShapeDtypeStruct((depth, d, GATE_COLS), BF16)],
        compiler_params=_params("arbitrary", "arbitrary"),
        name="regroup_w_in",
    )(w_in_t, w_in_t, w_in_t, w_in_t)


def _regroup_weights(w_in, w_uq, w_ukv, lru_w_a, lru_w_x):
    depth = w_in.shape[0]
    w_small, w_main, w_gate = regroup_w_in(jnp.swapaxes(w_in, 1, 2), tn=512)

    uq = w_uq.reshape(depth, Q_LORA, MLA_HEADS, QK_DIM)
    flat = lambda t: t.reshape(depth, t.shape[1], -1).astype(BF16)
    wqn = flat(uq[..., :QK_NOPE])
    wq1 = flat(uq[..., QK_NOPE:QK_NOPE + ROPE_HALF])
    wq2 = flat(uq[..., QK_NOPE + ROPE_HALF:])
    ukv = w_ukv.reshape(depth, KV_LORA, MLA_HEADS, QK_NOPE + V_HEAD)
    wk = flat(ukv[..., :QK_NOPE])
    wv = flat(ukv[..., QK_NOPE:])

    def pair_tiles(w):
        pairs = w.reshape(depth, LRU_BLOCKS // 2, 2, LRU_BLOCK_W, LRU_BLOCK_W)
        z = jnp.zeros_like(pairs[:, :, 0])
        top = jnp.concatenate([pairs[:, :, 0], z], axis=-1)
        bottom = jnp.concatenate([z, pairs[:, :, 1]], axis=-1)
        return jnp.concatenate([top, bottom], axis=-2)

    w_gates = jnp.concatenate([pair_tiles(lru_w_a), pair_tiles(lru_w_x)], axis=-1).astype(BF16)
    return w_main, w_gate, w_small, wqn, wq1, wq2, wk, wv, w_gates


def _rows(v):
    return v.reshape(v.shape[0], 1, -1)


def kernel(x, positions, norm_mix, w_in, mla_q_norm, mla_w_uq, mla_kv_norm, mla_w_ukv, lru_conv_w, lru_conv_b, lru_w_a, lru_b_a, lru_w_x, lru_b_x, lru_lambda, gdn_conv_w, gdn_a_log, gdn_dt_bias, gdn_norm, w_branch, b_gate, w_out, norm_ffn, ffn_w_gate, ffn_w_up, ffn_w_down, norm_final):
    b, s, d = x.shape
    assert b == 1
    depth = w_in.shape[0]
    xs = x.reshape(s, d)

    half = ROPE_HALF
    inv_freq = ROPE_THETA ** (-jnp.arange(half, dtype=F32) / half)
    inv_b = jnp.tile(inv_freq, MLA_HEADS).reshape(1, MLA_HEADS * half)
    pos_b = jnp.broadcast_to(positions.astype(F32).reshape(s, 1), (s, MLA_HEADS * half))
    cos, sin = rope_tables(pos_b, inv_b, tm=1024)

    lru_x_col, gdn_q_col = 0, 2
    ba_col = SMALL_ROPE_OFF // LANES

    w_main, w_gate, w_small, wqn, wq1, wq2, wk, wv, w_gates = _regroup_weights(
        w_in, mla_w_uq, mla_w_ukv, lru_w_a, lru_w_x)
    w_branch_b = w_branch.astype(BF16)
    w_out_b = w_out.astype(BF16)
    ffn_gate_b = ffn_w_gate.astype(BF16)
    ffn_up_b = ffn_w_up.astype(BF16)
    ffn_down_b = ffn_w_down.astype(BF16)
    gain_mix, gain_ffn = _rows(norm_mix), _rows(norm_ffn)
    gain_q, gain_kv = _rows(mla_q_norm), _rows(mla_kv_norm)
    lru_cb, lru_ba, lru_bx, lru_lam = _rows(lru_conv_b), _rows(lru_b_a), _rows(lru_b_x), _rows(lru_lambda)
    gdn_alog, gdn_dtb, gdn_nw = _rows(gdn_a_log), _rows(gdn_dt_bias), _rows(gdn_norm)
    gain_final = norm_final.reshape(1, d)

    for l in range(depth):
        proj_small, h = norm_matmul(xs, gain_mix, w_small, l, tm=1024, tn=SMALL_COLS)
        proj = matmul(h, w_main, l, tm=MAIN_TM, tn=MAIN_TN)

        q, k, v = mla_prep(proj_small, gain_q, gain_kv, wqn, wq1, wq2, wk, wv, cos, sin, l, tm=512)
        y_mla = attention(q, k, v, tr=1024, tk=1024, n_sub=2)
        y_lru = rglru(proj, lru_x_col, lru_x_col + 1, lru_conv_w, lru_cb, w_gates,
                      lru_ba, lru_bx, lru_lam, l, tm=512)
        y_gdn = gdn(proj, proj_small, gdn_q_col, ba_col, gdn_conv_w, gdn_alog, gdn_dtb, gdn_nw, l,
                    tm=512, chunks_per_step=4, exact_inverse=False)

        mixed = merge(h, y_mla, y_lru, y_gdn, w_gate, b_gate, w_branch_b, l, tm=MERGE_TM, tn=MERGE_TN)
        xs = residual_matmul(xs, mixed, w_out_b, l, tm=OUT_TM, tn=OUT_TN)
        xs = ffn(xs, gain_ffn, ffn_gate_b, ffn_up_b, ffn_down_b, gain_final, l, tm=FFN_TM, th=FFN_TH,
                 final_norm=(l == depth - 1))
    return xs.reshape(b, s, d)
```
